```python
import math, functools
import jax, jax.numpy as jnp
from jax import lax
import numpy as np

D_MODEL = 1024
BATCH = 4
SEQ = 8192
DEPTH = 2
DEC_BATCH = 32
DEC_SEQ = 1
PAST_LEN = 16384
PAGE_SIZE = 128

F32 = jnp.float32
N_META = 16
CHUNK = 128
EPS = 1e-6
RET_HEADS = 4
RET_DK = 128
RET_DV = 128
RET_W = RET_HEADS * RET_DV
ROPE_BASE = 10000.0
SB_HEADS = 4
SB_DH = 128
SB_W = SB_HEADS * SB_DH
SB_BIAS_HI = 5.0
SB_BIAS_LO = 9.5
M_INNER = D_MODEL
M_HEADDIM = 64
M_HEADS = M_INNER // M_HEADDIM
M_GROUPS = 4
M_REP = M_HEADS // M_GROUPS
M_STATE = 128
M_CONV = 4
M_XBC = M_INNER + 2 * M_GROUPS * M_STATE
SC_W = D_MODEL // 2
SC_CONV = 3
D_FF = 4 * D_MODEL
N_BRANCH = 4
IN_SPLITS = (RET_HEADS * RET_DK, RET_HEADS * RET_DK, RET_W, RET_W,
             SB_W, SB_W, SB_W,
             M_INNER, M_XBC, M_HEADS,
             SC_W, SC_W, SC_W,
             N_BRANCH * D_MODEL)
IN_WIDTH = sum(IN_SPLITS)
BRANCH_WIDTHS = (RET_W, SB_W, M_INNER, SC_W)
MIX_WIDTH = sum(BRANCH_WIDTHS)

kernel_name = 'hybrid_ret_sb_ssd_shortconv_step'


def _offsets(sizes):
    out, acc = [], 0
    for s in sizes[:-1]:
        acc += s
        out.append(acc)
    return out


def rmsnorm(x, w):
    xf = x.astype(F32)
    y = xf * lax.rsqrt(jnp.mean(xf * xf, axis=-1, keepdims=True) + EPS)
    return (y * w.astype(F32)).astype(x.dtype)


def rope(x, pos):
    half = x.shape[-1] // 2
    inv = ROPE_BASE ** (-jnp.arange(half, dtype=F32) / half)
    ang = pos.astype(F32)[:, None] * inv[None, :]
    cos = jnp.cos(ang)[None, :, None, :]
    sin = jnp.sin(ang)[None, :, None, :]
    x1, x2 = x[..., :half], x[..., half:]
    return jnp.concatenate([x1 * cos - x2 * sin, x1 * sin + x2 * cos], axis=-1)


def causal_dwconv(u, buf, w):
    k_w = w.shape[0]
    length = u.shape[1]
    full = jnp.concatenate([buf.astype(u.dtype), u], axis=1)
    out = full[:, 0:length] * w[0]
    for k in range(1, k_w):
        out = out + full[:, k:k + length] * w[k]
    return out, full[:, full.shape[1] - (k_w - 1):]


def chunked(chunk_fn, state, seqs):
    nb = seqs[0].shape[0]
    out_m, state = chunk_fn(state, *[a[:, :N_META] for a in seqs])
    nc = (seqs[0].shape[1] - N_META) // CHUNK

    def split(a):
        return jnp.moveaxis(a[:, N_META:].reshape((nb, nc, CHUNK) + a.shape[2:]), 1, 0)

    def step(s, inp):
        o, s = chunk_fn(s, *inp)
        return s, o

    state, out_r = lax.scan(step, state, tuple(split(a) for a in seqs))
    out_r = jnp.moveaxis(out_r, 0, 1)
    out_r = out_r.reshape((nb, nc * CHUNK) + out_r.shape[3:])
    return jnp.concatenate([out_m, out_r], axis=1), state


def retention_chunk(s, q, k, v):
    length = q.shape[1]
    lg = jnp.log1p(-jnp.exp2(-5.0 - jnp.arange(RET_HEADS, dtype=F32)))
    i = jnp.arange(length, dtype=F32)
    diff = i[:, None] - i[None, :]
    decay = jnp.where(diff >= 0, jnp.exp(lg[:, None, None] * jnp.maximum(diff, 0.0)), 0.0)
    scores = jnp.einsum('bihd,bjhd->bhij', q, k) * decay
    o = jnp.einsum('bhij,bjhe->bihe', scores, v)
    o = o + jnp.einsum('bihd,bhde->bihe', q, s) * jnp.exp(lg[None, :] * (i[:, None] + 1.0))[None, :, :, None]
    kd = k * jnp.exp(lg[None, :] * (length - 1.0 - i[:, None]))[None, :, :, None]
    s = jnp.exp(lg * length)[None, :, None, None] * s + jnp.einsum('bjhd,bjhe->bhde', kd, v)
    return o, s


def ssd_chunk(h, x, dt, bm, cm, a):
    length = x.shape[1]
    la = jnp.cumsum(dt * a, axis=1)
    causal = jnp.tril(jnp.ones((length, length), dtype=bool))[None, :, :, None, None]
    seg = la[:, :, None] - la[:, None, :]
    decay = jnp.where(causal, jnp.exp(jnp.where(causal, seg, 0.0)), 0.0)
    cb = jnp.einsum('bign,bjgn->bijg', cm, bm)
    wts = cb[..., None] * decay * dt[:, None]
    y = jnp.einsum('bijgr,bjgrp->bigrp', wts, x)
    y = y + jnp.einsum('bign,bgrpn->bigrp', cm, h) * jnp.exp(la)[..., None]
    w_end = jnp.exp(la[:, -1:] - la) * dt
    h = jnp.exp(la[:, -1])[..., None, None] * h + jnp.einsum('bjgr,bjgn,bjgrp->bgrpn', w_end, bm, x)
    return y, h


def sb_weights(z, mask):
    log_stay = jnp.where(mask, jax.nn.log_sigmoid(-z), 0.0)
    rest = lax.cumsum(log_stay, axis=z.ndim - 1, reverse=True) - log_stay
    return jnp.where(mask, jnp.exp(jax.nn.log_sigmoid(z) + rest), 0.0)


def sb_prompt(q, k, v, bias):
    nb, t_len, n_h, d = q.shape
    pos = jnp.arange(t_len)
    scale = d ** -0.5
    bias_f = bias.astype(F32)[None, :, None, None]

    def block(qb, qpos, kk, vv, kpos):
        z = jnp.einsum('bqhd,bkhd->bhqk', qb, kk, preferred_element_type=F32) * scale + bias_f
        a = sb_weights(z, kpos[None, :] < qpos[:, None])
        return jnp.einsum('bhqk,bkhd->bqhd', a, vv, preferred_element_type=F32)

    o_meta = block(q[:, :N_META], pos[:N_META], k[:, :N_META], v[:, :N_META], pos[:N_META])
    n_blk = (t_len - N_META) // CHUNK
    qb = jnp.moveaxis(q[:, N_META:].reshape(nb, n_blk, CHUNK, n_h, d), 1, 0)
    pb = pos[N_META:].reshape(n_blk, CHUNK)
    o_r = lax.map(lambda args: block(args[0], args[1], k, v, pos), (qb, pb))
    o_r = jnp.moveaxis(o_r, 0, 1).reshape(nb, n_blk * CHUNK, n_h, d)
    return jnp.concatenate([o_meta, o_r], axis=1)


def sb_sample(q, k, v, k_past, v_past, bias):
    past = k_past.shape[1]
    length = q.shape[1]
    scale = SB_DH ** -0.5
    z = jnp.concatenate([jnp.einsum('bqhd,bkhd->bhqk', q, k_past, preferred_element_type=F32),
                         jnp.einsum('bqhd,bkhd->bhqk', q, k, preferred_element_type=F32)], axis=-1) * scale
    z = z + bias.astype(F32)[None, :, None, None]
    kpos = jnp.arange(past + length)
    qpos = past + jnp.arange(length)
    a = sb_weights(z, kpos[None, :] < qpos[:, None])
    return (jnp.einsum('bhqk,bkhd->bqhd', a[..., :past], v_past, preferred_element_type=F32)
            + jnp.einsum('bhqk,bkhd->bqhd', a[..., past:], v, preferred_element_type=F32))


def token_mixers(h, pos, kv_past, s_ret, s_ssm, s_mconv, s_sconv,
                 w_in_l, w_branch_l, w_o_l, ret_gn_w_l, m_conv_w_l, m_conv_b_l,
                 m_dt_bias_l, m_a_log_l, m_d_l, m_norm_w_l, sc_w_l, sb_bias_l):
    prompt = kv_past is None
    nb, length, _ = h.shape
    run = chunked if prompt else (lambda fn, s, seqs: fn(s, *seqs))
    w_parts = jnp.split(w_in_l, _offsets(IN_SPLITS), axis=1)
    (rq, rk, rv, rg, sq, sk, sv, mz, mxbc, mdt, gb, gc, gh, gl) = [
        jnp.einsum('bld,de->ble', h, w, preferred_element_type=F32) for w in w_parts]

    rq = rope(rq.reshape(nb, length, RET_HEADS, RET_DK), pos) * (RET_DK ** -0.5)
    rk = rope(rk.reshape(nb, length, RET_HEADS, RET_DK), pos)
    rv = rv.reshape(nb, length, RET_HEADS, RET_DV)
    o, s_ret = run(retention_chunk, s_ret.astype(F32), (rq, rk, rv))
    mu = jnp.mean(o, axis=-1, keepdims=True)
    var = jnp.mean(jnp.square(o - mu), axis=-1, keepdims=True)
    o = ((o - mu) * lax.rsqrt(var + EPS)).reshape(nb, length, RET_W)
    y_ret = o * ret_gn_w_l.astype(F32) * jax.nn.silu(rg)

    sq = sq.reshape(nb, length, SB_HEADS, SB_DH)
    sk = sk.reshape(nb, length, SB_HEADS, SB_DH)
    sv = sv.reshape(nb, length, SB_HEADS, SB_DH)
    if prompt:
        o = sb_prompt(sq, sk, sv, sb_bias_l)
    else:
        o = sb_sample(sq, sk, sv, kv_past[0], kv_past[1], sb_bias_l)
    y_sb = o.reshape(nb, length, SB_W)

    xbc, s_mconv = causal_dwconv(mxbc, s_mconv, m_conv_w_l)
    xbc = jax.nn.silu(xbc + m_conv_b_l)
    mx, mb, mc = jnp.split(xbc, [M_INNER, M_INNER + M_GROUPS * M_STATE], axis=-1)
    mx = mx.reshape(nb, length, M_GROUPS, M_REP, M_HEADDIM)
    mb = mb.reshape(nb, length, M_GROUPS, M_STATE)
    mc = mc.reshape(nb, length, M_GROUPS, M_STATE)
    dt = jax.nn.softplus(mdt + m_dt_bias_l).reshape(nb, length, M_GROUPS, M_REP)
    a = -jnp.exp(m_a_log_l.astype(F32)).reshape(M_GROUPS, M_REP)
    s0 = s_ssm.astype(F32).reshape(nb, M_GROUPS, M_REP, M_HEADDIM, M_STATE)
    y, s_ssm = run(functools.partial(ssd_chunk, a=a), s0, (mx, dt, mb, mc))
    y = y + m_d_l.astype(F32).reshape(M_GROUPS, M_REP)[:, :, None] * mx
    y = y.reshape(nb, length, M_INNER) * jax.nn.silu(mz)
    y_ssm = rmsnorm(y, m_norm_w_l)
    s_ssm = s_ssm.reshape(nb, M_HEADS, M_HEADDIM, M_STATE)

    u, s_sconv = causal_dwconv(gc * gh, s_sconv, sc_w_l)
    y_sc = gb * u

    gates = jax.nn.sigmoid(gl.reshape(nb, length, N_BRANCH, D_MODEL))
    branches = (y_ret, y_sb, y_ssm, y_sc)
    offs = [0] + _offsets(BRANCH_WIDTHS + (0,))
    merged = gates[:, :, 0] * jnp.einsum('blc,cd->bld', branches[0], w_branch_l[offs[0]:offs[1]], preferred_element_type=F32)
    for i in range(1, N_BRANCH):
        merged = merged + gates[:, :, i] * jnp.einsum('blc,cd->bld', branches[i], w_branch_l[offs[i]:offs[i + 1]], preferred_element_type=F32)
    out = jnp.einsum('bld,de->ble', merged, w_o_l, preferred_element_type=F32)
    return out, (sk, sv, s_ret, s_ssm, s_mconv, s_sconv)


def trunk_layer(x, pos, kv_past, s_ret, s_ssm, s_mconv, s_sconv, nw, w_up_l, w_down_l, *mix_w):
    h = rmsnorm(x, nw[0])
    m, st = token_mixers(h, pos, kv_past, s_ret, s_ssm, s_mconv, s_sconv, *mix_w)
    x = x + rmsnorm(m, nw[1]).astype(x.dtype)
    h = rmsnorm(x, nw[2])
    f = jnp.square(jax.nn.relu(jnp.einsum('bld,df->blf', h, w_up_l, preferred_element_type=F32)))
    f = jnp.einsum('blf,fd->bld', f, w_down_l, preferred_element_type=F32)
    x = x + rmsnorm(f, nw[3]).astype(x.dtype)
    return x, st


def setup_inputs(seed: int = 0) -> dict:
    key = jax.random.key(seed)
    ks = jax.random.split(key, 26)
    n_pages = PAST_LEN // PAGE_SIZE
    n_used = DEC_BATCH * n_pages
    n_phys = n_used + (n_used + 3) // 4

    def nrm(k, shape, s):
        return s * jax.random.normal(k, shape, F32)

    dt0 = jnp.exp(jax.random.uniform(ks[20], (DEPTH, M_HEADS), F32, math.log(1e-3), math.log(1e-1)))
    page_table = jax.random.permutation(ks[22], n_phys)[:n_used].reshape(DEC_BATCH, n_pages).astype(jnp.int32)
    sb_bias0 = -jnp.linspace(SB_BIAS_HI, SB_BIAS_LO, SB_HEADS, dtype=F32)[None, :]
    return {
        'x_prompt': nrm(ks[0], (BATCH, SEQ, D_MODEL), 1.0),
        'x_sample': nrm(ks[1], (DEC_BATCH, DEC_SEQ, D_MODEL), 1.0),
        'cache_sb_k': nrm(ks[2], (DEPTH, n_phys, PAGE_SIZE, SB_HEADS, SB_DH), 1.0),
        'cache_sb_v': nrm(ks[3], (DEPTH, n_phys, PAGE_SIZE, SB_HEADS, SB_DH), 1.0),
        'state_ret': nrm(ks[4], (DEPTH, DEC_BATCH, RET_HEADS, RET_DK, RET_DV), 1.0),
        'state_ssm': nrm(ks[5], (DEPTH, DEC_BATCH, M_HEADS, M_HEADDIM, M_STATE), 0.1),
        'state_mconv': nrm(ks[6], (DEPTH, DEC_BATCH, M_CONV - 1, M_XBC), 1.0),
        'state_sconv': nrm(ks[7], (DEPTH, DEC_BATCH, SC_CONV - 1, SC_W), 1.0),
        'page_table': page_table,
        'meta': nrm(ks[8], (N_META, D_MODEL), 1.0),
        'w_in': nrm(ks[9], (DEPTH, D_MODEL, IN_WIDTH), D_MODEL ** -0.5),
        'w_branch': nrm(ks[10], (DEPTH, MIX_WIDTH, D_MODEL), SC_W ** -0.5),
        'w_o': nrm(ks[11], (DEPTH, D_MODEL, D_MODEL), D_MODEL ** -0.5),
        'w_up': nrm(ks[12], (DEPTH, D_MODEL, D_FF), D_MODEL ** -0.5),
        'w_down': nrm(ks[13], (DEPTH, D_FF, D_MODEL), D_FF ** -0.5),
        'norm_w': 1.0 + nrm(ks[14], (DEPTH, 4, D_MODEL), 0.1),
        'ret_gn_w': 1.0 + nrm(ks[15], (DEPTH, RET_W), 0.1),
        'm_conv_w': nrm(ks[16], (DEPTH, M_CONV, M_XBC), M_CONV ** -0.5),
        'm_conv_b': nrm(ks[17], (DEPTH, M_XBC), 0.1),
        'm_dt_bias': dt0 + jnp.log(-jnp.expm1(-dt0)),
        'm_a_log': jnp.log(jax.random.uniform(ks[21], (DEPTH, M_HEADS), F32, 1.0, 16.0)),
        'm_d': 1.0 + nrm(ks[18], (DEPTH, M_HEADS), 0.1),
        'm_norm_w': 1.0 + nrm(ks[19], (DEPTH, M_INNER), 0.1),
        'sc_w': nrm(ks[23], (DEPTH, SC_CONV, SC_W), SC_CONV ** -0.5),
        'sb_bias': sb_bias0 + nrm(ks[24], (DEPTH, SB_HEADS), 0.1),
    }


def reference(x_prompt, x_sample, cache_sb_k, cache_sb_v, state_ret, state_ssm, state_mconv,
              state_sconv, page_table, meta, w_in, w_branch, w_o, w_up, w_down, norm_w, ret_gn_w,
              m_conv_w, m_conv_b, m_dt_bias, m_a_log, m_d, m_norm_w, sc_w, sb_bias):
    nb_p = x_prompt.shape[0]
    nb_s, dec_len, _ = x_sample.shape
    n_pages = page_table.shape[1]
    past = n_pages * PAGE_SIZE
    xp = jnp.concatenate([jnp.broadcast_to(meta.astype(x_prompt.dtype)[None], (nb_p, N_META, D_MODEL)), x_prompt], axis=1)
    pos_p = jnp.arange(xp.shape[1])
    pos_s = past + jnp.arange(dec_len)
    xs = x_sample
    zr = jnp.zeros((nb_p, RET_HEADS, RET_DK, RET_DV), F32)
    zs = jnp.zeros((nb_p, M_HEADS, M_HEADDIM, M_STATE), F32)
    zc = jnp.zeros((nb_p, M_CONV - 1, M_XBC), F32)
    zd = jnp.zeros((nb_p, SC_CONV - 1, SC_W), F32)
    p_out = [[] for _ in range(6)]
    s_out = [[] for _ in range(6)]
    for l in range(DEPTH):
        mix_w = (w_in[l], w_branch[l], w_o[l], ret_gn_w[l], m_conv_w[l], m_conv_b[l],
                 m_dt_bias[l], m_a_log[l], m_d[l], m_norm_w[l], sc_w[l], sb_bias[l])
        xp, st_p = trunk_layer(xp, pos_p, None, zr, zs, zc, zd, norm_w[l], w_up[l], w_down[l], *mix_w)
        k_past = cache_sb_k[l][page_table].reshape(nb_s, past, SB_HEADS, SB_DH)
        v_past = cache_sb_v[l][page_table].reshape(nb_s, past, SB_HEADS, SB_DH)
        xs, st_s = trunk_layer(xs, pos_s, (k_past, v_past), state_ret[l], state_ssm[l], state_mconv[l],
                               state_sconv[l], norm_w[l], w_up[l], w_down[l], *mix_w)
        for i in range(6):
            p_out[i].append(st_p[i].astype(x_prompt.dtype))
            s_out[i].append(st_s[i].astype(x_sample.dtype))
    p_sb_k, p_sb_v, p_ret, p_ssm, p_mconv, p_sconv = [jnp.stack(a) for a in p_out]
    s_sb_k, s_sb_v, s_ret, s_ssm, s_mconv, s_sconv = [jnp.stack(a) for a in s_out]
    y_prompt = xp[:, N_META:]
    return (y_prompt, xs, p_sb_k, p_sb_v, p_ret, p_ssm, p_mconv, p_sconv,
            s_sb_k, s_sb_v, s_ret, s_ssm, s_mconv, s_sconv)
```

```python
import functools
import math

import jax
import jax.numpy as jnp
from jax import lax
from jax.experimental import pallas as pl
from jax.experimental.pallas import tpu as pltpu

F32 = jnp.float32
BF16 = jnp.bfloat16
HIGHEST = lax.Precision.HIGHEST

N_META = 16
CHUNK = 128
EPS = 1e-6
RET_HEADS = 4
RET_DK = 128
ROPE_BASE = 10000.0
SB_HEADS = 4
SB_DH = 128
M_HEADDIM = 64
M_GROUPS = 4
M_STATE = 128
M_CONV = 4
SC_CONV = 3
N_BRANCH = 4
SAMPLE_CHUNK = 16
LANES = 128
VMEM_LIMIT = 56 * 1024 * 1024

COL_GL, COL_XBC, COL_MZ = 0, 4096, 6144
COL_RQ, COL_RK, COL_RV, COL_RG = 7168, 7680, 8192, 8704
COL_SQ, COL_SK, COL_SV = 9216, 9728, 10240
COL_GB, COL_GC, COL_GH = 10752, 11264, 11776
PROJ_W = 12288

NT = (((1,), (1,)), ((), ()))
TN = (((0,), (0,)), ((), ()))


def _pick_tile(m, cap, mult):
    best = None
    for t in range(mult, min(m, cap) + 1, mult):
        if m % t == 0:
            best = t
    assert best is not None, (m, cap, mult)
    return best


def _sigmoid(x):
    return 1.0 / (1.0 + jnp.exp(-x))


def _softplus(x):
    return jnp.maximum(x, 0.0) + jnp.log1p(jnp.exp(-jnp.abs(x)))


def _rms(x, w):
    return x * lax.rsqrt(jnp.mean(x * x, axis=-1, keepdims=True) + EPS) * w


def _params(sem):
    return pltpu.CompilerParams(dimension_semantics=sem, vmem_limit_bytes=VMEM_LIMIT)


def _inproj_kernel(x_ref, nw_ref, w_ref, wdt_ref, out_ref, dt_ref, h_scr):
    @pl.when(pl.program_id(1) == 0)
    def _():
        hb = _rms(x_ref[...], nw_ref[...]).astype(BF16)
        h_scr[...] = hb
        dt_ref[...] = jnp.dot(hb, wdt_ref[...], preferred_element_type=F32)

    out_ref[...] = jnp.dot(h_scr[...], w_ref[...], preferred_element_type=F32)


def _inproj(x, nw, w_main, w_dt):
    m, d = x.shape
    tm = _pick_tile(m, 1280, SAMPLE_CHUNK)
    tn = 1024
    return pl.pallas_call(
        _inproj_kernel,
        grid=(m // tm, PROJ_W // tn),
        in_specs=[
            pl.BlockSpec((tm, d), lambda i, j: (i, 0)),
            pl.BlockSpec((1, d), lambda i, j: (0, 0)),
            pl.BlockSpec((d, tn), lambda i, j: (0, j)),
            pl.BlockSpec((d, LANES), lambda i, j: (0, 0)),
        ],
        out_specs=[
            pl.BlockSpec((tm, tn), lambda i, j: (i, j)),
            pl.BlockSpec((tm, LANES), lambda i, j: (i, 0)),
        ],
        out_shape=[jax.ShapeDtypeStruct((m, PROJ_W), F32), jax.ShapeDtypeStruct((m, LANES), F32)],
        scratch_shapes=[pltpu.VMEM((tm, d), BF16)],
        compiler_params=_params(("parallel", "arbitrary")),
        name="inproj",
    )(x, nw, w_main, w_dt)


def _ret_kernel(nv, lc, q_ref, k_ref, v_ref, g_ref, cos_ref, sin_ref, gnw_ref, s0_ref, y_ref, s_ref):
    c = pl.program_id(1)

    @pl.when(c == 0)
    def _():
        s_ref[...] = s0_ref[...]

    row = lax.broadcasted_iota(jnp.int32, (lc, LANES), 0)
    rowf = row.astype(F32)
    valid = jnp.logical_or(c > 0, row < nv)
    n_valid = jnp.where(c == 0, nv, lc).astype(F32)
    l_eff = jnp.full((lc, LANES), 1.0, F32) * n_valid
    l_eff_sq = jnp.full((LANES, LANES), 1.0, F32) * n_valid
    diff = (lax.broadcasted_iota(jnp.int32, (lc, lc), 0)
            - lax.broadcasted_iota(jnp.int32, (lc, lc), 1)).astype(F32)
    cos = cos_ref[...]
    sin = sin_ref[...]
    for h in range(RET_HEADS):
        lg = math.log1p(-(2.0 ** (-5 - h)))
        sl = slice(h * LANES, (h + 1) * LANES)
        q = jnp.where(valid, q_ref[:, sl], 0.0)
        k = jnp.where(valid, k_ref[:, sl], 0.0)
        v = jnp.where(valid, v_ref[:, sl], 0.0).astype(BF16)
        qr = ((q * cos + pltpu.roll(q, 64, 1) * sin) * (RET_DK ** -0.5)).astype(BF16)
        kr = k * cos + pltpu.roll(k, 64, 1) * sin
        decay = jnp.where(diff >= 0, jnp.exp(lg * jnp.maximum(diff, 0.0)), 0.0)
        scores = lax.dot_general(qr, kr.astype(BF16), NT, preferred_element_type=F32) * decay
        s = s_ref[h]
        o = jnp.dot(scores.astype(BF16), v, preferred_element_type=F32)
        o = o + jnp.dot(qr, s.astype(BF16), preferred_element_type=F32) * jnp.exp(lg * (rowf + 1.0))
        kd = (kr * jnp.exp(lg * (l_eff - 1.0 - rowf))).astype(BF16)
        s_ref[h] = jnp.exp(lg * l_eff_sq) * s + lax.dot_general(kd, v, TN, preferred_element_type=F32)
        mu = jnp.mean(o, axis=-1, keepdims=True)
        d = o - mu
        var = jnp.mean(d * d, axis=-1, keepdims=True)
        g = g_ref[:, sl]
        y = d * lax.rsqrt(var + EPS) * gnw_ref[:, sl] * (g * _sigmoid(g))
        y_ref[:, sl] = y.astype(BF16)


def _retention(proj, cos2, sin2, gnw, s0, nb, nc, lc, nv):
    m = proj.shape[0]
    w = RET_HEADS * LANES

    def col(off):
        return pl.BlockSpec((lc, w), lambda b, c: (b * nc + c, off // w))

    return pl.pallas_call(
        functools.partial(_ret_kernel, nv, lc),
        grid=(nb, nc),
        in_specs=[
            col(COL_RQ), col(COL_RK), col(COL_RV), col(COL_RG),
            pl.BlockSpec((lc, LANES), lambda b, c: (c, 0)),
            pl.BlockSpec((lc, LANES), lambda b, c: (c, 0)),
            pl.BlockSpec((1, w), lambda b, c: (0, 0)),
            pl.BlockSpec((None, RET_HEADS, LANES, LANES), lambda b, c: (b, 0, 0, 0)),
        ],
        out_specs=[
            pl.BlockSpec((lc, w), lambda b, c: (b * nc + c, 0)),
            pl.BlockSpec((None, RET_HEADS, LANES, LANES), lambda b, c: (b, 0, 0, 0)),
        ],
        out_shape=[jax.ShapeDtypeStruct((m, w), BF16),
                   jax.ShapeDtypeStruct((nb, RET_HEADS, LANES, LANES), F32)],
        compiler_params=_params(("parallel", "arbitrary")),
        name="retention",
    )(proj, proj, proj, proj, cos2, sin2, gnw, s0)


def _cumsum_matrix(n):
    j = lax.broadcasted_iota(jnp.int32, (n, 2 * n), 0)
    s = lax.broadcasted_iota(jnp.int32, (n, 2 * n), 1)
    return jnp.where(jnp.logical_or(j > s, s >= n), 1.0, 0.0).astype(BF16)


def _split_dot(x, w):
    hi = x.astype(BF16)
    lo = (x - hi.astype(F32)).astype(BF16)
    return (jnp.dot(hi, w, preferred_element_type=F32)
            + jnp.dot(lo, w, preferred_element_type=F32))


def _sbp_kernel(nv, bias_ref, q_ref, k_ref, v_ref, o_ref):
    h = pl.program_id(1)
    qi = pl.program_id(2)
    bias = bias_ref[h]
    scale = SB_DH ** -0.5
    q = q_ref[...].astype(BF16)
    ri = lax.broadcasted_iota(jnp.int32, (CHUNK, CHUNK), 0)
    ci = lax.broadcasted_iota(jnp.int32, (CHUNK, CHUNK), 1)
    u = _cumsum_matrix(CHUNK)

    def pair(j, carry, acc, mask):
        start = j * CHUNK if isinstance(j, int) else pl.multiple_of(j * CHUNK, CHUNK)
        kb = k_ref[pl.ds(start, CHUNK), :].astype(BF16)
        vb = v_ref[pl.ds(start, CHUNK), :].astype(BF16)
        z = lax.dot_general(q, kb, NT, preferred_element_type=F32) * scale + bias
        ls = -_softplus(z)
        lsm = ls if mask is None else jnp.where(mask, ls, 0.0)
        r = _split_dot(lsm, u)
        a = jnp.exp(z + ls + r[:, :CHUNK] + carry)
        if mask is not None:
            a = jnp.where(mask, a, 0.0)
        acc = acc + jnp.dot(a.astype(BF16), vb, preferred_element_type=F32)
        return carry + r[:, CHUNK:], acc

    zeros = jnp.zeros((CHUNK, CHUNK), F32)
    diag_mask = jnp.logical_and(ci < ri, jnp.logical_or(qi > 0, ci < nv))
    carry, acc = pair(qi, zeros, zeros, diag_mask)

    def body(t, ca):
        return pair(qi - 1 - t, ca[0], ca[1], None)

    carry, acc = lax.fori_loop(0, jnp.maximum(qi - 1, 0), body, (carry, acc))
    first_mask = jnp.logical_and(ci < nv, qi > 0)
    carry, acc = pair(0, carry, acc, first_mask)
    o_ref[...] = acc.astype(BF16)


def _sb_prompt(proj, bias, nb, nc, nv):
    m = proj.shape[0]
    t_pad = nc * CHUNK
    return pl.pallas_call(
        functools.partial(_sbp_kernel, nv),
        grid=(nb, SB_HEADS, nc),
        in_specs=[
            pl.BlockSpec(memory_space=pltpu.SMEM),
            pl.BlockSpec((CHUNK, SB_DH), lambda b, h, q: (b * nc + q, COL_SQ // SB_DH + h)),
            pl.BlockSpec((t_pad, SB_DH), lambda b, h, q: (b, COL_SK // SB_DH + h)),
            pl.BlockSpec((t_pad, SB_DH), lambda b, h, q: (b, COL_SV // SB_DH + h)),
        ],
        out_specs=pl.BlockSpec((CHUNK, SB_DH), lambda b, h, q: (b * nc + q, h)),
        out_shape=jax.ShapeDtypeStruct((m, SB_HEADS * SB_DH), BF16),
        compiler_params=_params(("parallel", "parallel", "arbitrary")),
        name="sb_prompt",
    )(bias, proj, proj, proj)


SBS_PAGES = 8
HEAD_BITS = SB_HEADS.bit_length() - 1
assert 1 << HEAD_BITS == SB_HEADS


def _head_of(x):
    return jnp.bitwise_and(x, SB_HEADS - 1)


def _sbs_constants(page_w):
    j = jnp.arange(page_w)[:, None]
    t = jnp.arange(2 * page_w)[None, :]
    u = (j % SB_HEADS == t % SB_HEADS) & ((j > t) | (t >= page_w))
    rows = SB_HEADS * SBS_PAGES
    a = jnp.arange(2 * rows)[:, None]
    b = jnp.arange(rows)[None, :]
    mix = (a % SB_HEADS == b % SB_HEADS) & ((a >= rows) | (b // SB_HEADS < a // SB_HEADS))
    return u.astype(BF16), mix.astype(F32)


def _sbs_kernel(lc, n_steps, past, dec_len, pt_ref, bias_ref, q_ref, kn_ref, vn_ref, u_ref, mix_ref, *refs):
    del pt_ref
    pp = SBS_PAGES
    k_refs, v_refs = refs[:pp], refs[pp:2 * pp]
    o_ref, acc_ref, carry_ref = refs[2 * pp:]
    s = pl.program_id(1)
    rows = SB_HEADS * pp
    page_w = k_refs[0].shape[0]
    scale = SB_DH ** -0.5
    rr = lax.broadcasted_iota(jnp.int32, (rows, page_w), 0)
    nn = lax.broadcasted_iota(jnp.int32, (rows, page_w), 1)
    r8 = lax.broadcasted_iota(jnp.int32, (8, SB_DH), 0)

    @pl.when(s == 0)
    def _():
        kpos = past + lax.broadcasted_iota(jnp.int32, (1, 1), 1)
        visible = kpos < past + dec_len - 1
        acc0 = jnp.zeros((8, SB_DH), F32)
        carry0 = jnp.zeros((rows, page_w), F32)
        for h in range(SB_HEADS):
            sl = slice(h * SB_DH, (h + 1) * SB_DH)
            z_new = (jnp.sum(q_ref[0:1, sl] * kn_ref[0:1, sl], axis=1, keepdims=True) * scale
                     + bias_ref[h])
            ls_new = -_softplus(z_new)
            a_new = jnp.where(visible, jnp.exp(z_new + ls_new), 0.0)
            acc0 = jnp.where(r8 == h, a_new * vn_ref[0:1, sl], acc0)
            carry0 = jnp.where(_head_of(rr) == h, jnp.where(visible, ls_new, 0.0), carry0)
        acc_ref[...] = acc0
        carry_ref[...] = carry0

    r_q = lax.broadcasted_iota(jnp.int32, (rows, SB_DH), 0)
    qrep = jnp.zeros((rows, SB_DH), F32)
    for h in range(SB_HEADS):
        qrep = jnp.where(_head_of(r_q) == h, q_ref[0:1, h * SB_DH:(h + 1) * SB_DH], qrep)

    z = jnp.zeros((rows, page_w), F32)
    for i in range(pp):
        qsel = jnp.where(lax.shift_right_logical(r_q, HEAD_BITS) == i, qrep, 0.0).astype(BF16)
        z = z + lax.dot_general(qsel, k_refs[i][...].astype(BF16), NT, preferred_element_type=F32)

    own = _head_of(nn) == _head_of(rr)
    bias = jnp.zeros((rows, page_w), F32)
    for h in range(SB_HEADS):
        bias = jnp.where(_head_of(rr) == h, bias_ref[h], bias)
    z = z * scale + bias
    ls = jnp.where(own, -_softplus(z), 0.0)
    r = _split_dot(ls, u_ref[...])
    rest, tot = r[:, :page_w], r[:, page_w:]
    pt = jnp.dot(mix_ref[...], tot, precision=HIGHEST, preferred_element_type=F32)
    carry = carry_ref[...]
    a = jnp.where(own, jnp.exp(z + ls + rest + pt[:rows] + carry), 0.0).astype(BF16)
    carry_ref[...] = carry + pt[rows:]

    acc = acc_ref[...]
    for g in range(pp // 2):
        a8 = a[8 * g:8 * g + 8]
        o0 = jnp.dot(a8, v_refs[2 * g][...].astype(BF16), preferred_element_type=F32)
        o1 = jnp.dot(a8, v_refs[2 * g + 1][...].astype(BF16), preferred_element_type=F32)
        acc = acc + jnp.where(r8 < SB_HEADS, o0, o1)
    acc_ref[...] = acc

    @pl.when(s == n_steps - 1)
    def _():
        tot8 = acc_ref[...]
        o4 = tot8[0:SB_HEADS] + tot8[SB_HEADS:2 * SB_HEADS]
        row0 = lax.broadcasted_iota(jnp.int32, (lc, SB_DH), 0) == 0
        for h in range(SB_HEADS):
            o_ref[:, h * SB_DH:(h + 1) * SB_DH] = jnp.where(row0, o4[h:h + 1], 0.0).astype(BF16)


def _sb_sample(proj, bias, cache_k, cache_v, page_table, layer, nb, lc, dec_len):
    m = proj.shape[0]
    n_pages = page_table.shape[1]
    pp = SBS_PAGES
    assert n_pages % pp == 0 and pp % 2 == 0 and SB_HEADS * 2 == 8
    n_steps = n_pages // pp
    page_rows = cache_k.shape[2]
    past = n_pages * (page_rows // SB_HEADS)
    width = SB_HEADS * SB_DH
    u, mix = _sbs_constants(page_rows)

    def page_spec(i):
        return pl.BlockSpec((None, None, page_rows, SB_DH),
                            lambda b, s, pt: (layer, pt[b, n_pages - 1 - (s * pp + i)], 0, 0))

    def col(off):
        return pl.BlockSpec((lc, width), lambda b, s, pt: (b, off // width))

    grid_spec = pltpu.PrefetchScalarGridSpec(
        num_scalar_prefetch=1,
        grid=(nb, n_steps),
        in_specs=[pl.BlockSpec(memory_space=pltpu.SMEM), col(COL_SQ), col(COL_SK), col(COL_SV),
                  pl.BlockSpec(u.shape, lambda b, s, pt: (0, 0)),
                  pl.BlockSpec(mix.shape, lambda b, s, pt: (0, 0))]
                 + [page_spec(i) for i in range(pp)] + [page_spec(i) for i in range(pp)],
        out_specs=pl.BlockSpec((lc, width), lambda b, s, pt: (b, 0)),
        scratch_shapes=[pltpu.VMEM((8, SB_DH), F32), pltpu.VMEM((SB_HEADS * pp, page_rows), F32)],
    )
    return pl.pallas_call(
        functools.partial(_sbs_kernel, lc, n_steps, past, dec_len),
        grid_spec=grid_spec,
        out_shape=jax.ShapeDtypeStruct((m, width), BF16),
        compiler_params=_params(("parallel", "arbitrary")),
        name="sb_sample",
    )(page_table, bias, proj, proj, proj, u, mix, *([cache_k] * pp), *([cache_v] * pp))


def _conv_taps(buf, w_ref, lc, taps):
    base = 8 - (taps - 1)
    out = buf[base:base + lc, :] * w_ref[0:1, :]
    for k in range(1, taps):
        out = out + buf[base + k:base + k + lc, :] * w_ref[k:k + 1, :]
    return out


def _roll_tail(buf, c, nv, lc, taps):
    base = 8 - (taps - 1)

    @pl.when(c == 0)
    def _():
        buf[base:8, :] = buf[base + nv:8 + nv, :]

    @pl.when(c > 0)
    def _():
        buf[base:8, :] = buf[base + lc:8 + lc, :]


def _ssd_kernel(nv, lc, n_chunks,
                xbc_ref, mz_ref, dt_ref, gb_ref, gc_ref, gh_ref,
                cw_ref, cb_ref, dtb_ref, alog_ref, dfull_ref, nw_ref, scw_ref,
                mc0_ref, h0_ref, sc0_ref,
                yssm_ref, ysc_ref, mc_ref, h_ref, sc_ref,
                cbuf, sbuf):
    c = pl.program_id(1)
    n_heads = h_ref.shape[0]
    m_inner = n_heads * M_HEADDIM
    rep = n_heads // M_GROUPS

    @pl.when(c == 0)
    def _():
        cbuf[8 - (M_CONV - 1):8, :] = mc0_ref[...]
        sbuf[8 - (SC_CONV - 1):8, :] = sc0_ref[...]
        h_ref[...] = h0_ref[...]

    cbuf[8:8 + lc, :] = xbc_ref[...]
    conv = _conv_taps(cbuf, cw_ref, lc, M_CONV) + cb_ref[...]
    _roll_tail(cbuf, c, nv, lc, M_CONV)
    xc = conv * _sigmoid(conv)
    mx = xc[:, :m_inner]
    mb = xc[:, m_inner:m_inner + M_GROUPS * M_STATE].astype(BF16)
    mc = xc[:, m_inner + M_GROUPS * M_STATE:].astype(BF16)

    row = lax.broadcasted_iota(jnp.int32, (lc, LANES), 0)
    valid = jnp.logical_or(c > 0, row < nv)
    dt = jnp.where(valid, _softplus(dt_ref[...] + dtb_ref[...]), 0.0)
    dta = dt * (-jnp.exp(alog_ref[...]))
    ii = lax.broadcasted_iota(jnp.int32, (lc, lc), 0)
    jj = lax.broadcasted_iota(jnp.int32, (lc, lc), 1)
    causal = ii >= jj
    la = jnp.dot(jnp.where(causal, 1.0, 0.0), dta, precision=HIGHEST, preferred_element_type=F32)

    def transposed(a):
        if lc < LANES:
            a = jnp.concatenate([a, jnp.zeros((LANES - lc, LANES), F32)], axis=0)
        return a.T[:, :lc]

    la_t = transposed(la)
    dt_t = transposed(dt)
    la_last = la[lc - 1:lc, :]
    ela = jnp.exp(la)
    w_end = jnp.exp(la_last - la) * dt
    e_last = jnp.exp(la_last)

    ys = []
    for g in range(M_GROUPS):
        bm = mb[:, g * M_STATE:(g + 1) * M_STATE]
        cm = mc[:, g * M_STATE:(g + 1) * M_STATE]
        cb = lax.dot_general(cm, bm, NT, preferred_element_type=F32)
        for r in range(rep):
            hd = g * rep + r
            seg = la[:, hd:hd + 1] - la_t[hd:hd + 1, :]
            dec = jnp.where(causal, jnp.exp(jnp.where(causal, seg, 0.0)), 0.0)
            wts = (cb * dec * dt_t[hd:hd + 1, :]).astype(BF16)
            x_h = mx[:, hd * M_HEADDIM:(hd + 1) * M_HEADDIM]
            hst = h_ref[hd]
            y_h = jnp.dot(wts, x_h.astype(BF16), preferred_element_type=F32)
            y_h = y_h + (lax.dot_general(cm, hst.astype(BF16), NT, preferred_element_type=F32)
                         * ela[:, hd:hd + 1])
            xw = (x_h * w_end[:, hd:hd + 1]).astype(BF16)
            h_ref[hd] = (e_last[:, hd:hd + 1] * hst
                         + lax.dot_general(xw, bm, TN, preferred_element_type=F32))
            ys.append(y_h)
    y = jnp.concatenate(ys, axis=1) + dfull_ref[...] * mx
    mz = mz_ref[...]
    y = y * (mz * _sigmoid(mz))
    yssm_ref[...] = _rms(y, nw_ref[...]).astype(BF16)

    sbuf[8:8 + lc, :] = gc_ref[...] * gh_ref[...]
    ysc_ref[...] = (gb_ref[...] * _conv_taps(sbuf, scw_ref, lc, SC_CONV)).astype(BF16)
    _roll_tail(sbuf, c, nv, lc, SC_CONV)

    @pl.when(c == n_chunks - 1)
    def _():
        mc_ref[...] = cbuf[8 - (M_CONV - 1):8, :]
        sc_ref[...] = sbuf[8 - (SC_CONV - 1):8, :]


def _ssd(proj, dtp, cw, cb, dtb, alog, dfull, nw, scw, mc0, h0, sc0, nb, nc, lc, nv):
    m = proj.shape[0]
    n_heads, m_xbc, sc_w = h0.shape[1], cw.shape[1], scw.shape[1]
    m_inner = n_heads * M_HEADDIM

    def col(off, w):
        return pl.BlockSpec((lc, w), lambda b, c: (b * nc + c, off // w))

    def full(a):
        return pl.BlockSpec(a.shape, lambda b, c: (0,) * a.ndim)

    def per_seq(a):
        return pl.BlockSpec((None,) + a.shape[1:], lambda b, c: (b,) + (0,) * (a.ndim - 1))

    outs = [jax.ShapeDtypeStruct((m, m_inner), BF16), jax.ShapeDtypeStruct((m, sc_w), BF16),
            jax.ShapeDtypeStruct(mc0.shape, F32), jax.ShapeDtypeStruct(h0.shape, F32),
            jax.ShapeDtypeStruct(sc0.shape, F32)]
    return pl.pallas_call(
        functools.partial(_ssd_kernel, nv, lc, nc),
        grid=(nb, nc),
        in_specs=[col(COL_XBC, m_xbc), col(COL_MZ, m_inner),
                  pl.BlockSpec((lc, LANES), lambda b, c: (b * nc + c, 0)),
                  col(COL_GB, sc_w), col(COL_GC, sc_w), col(COL_GH, sc_w),
                  full(cw), full(cb), full(dtb), full(alog), full(dfull), full(nw), full(scw),
                  per_seq(mc0), per_seq(h0), per_seq(sc0)],
        out_specs=[pl.BlockSpec((lc, m_inner), lambda b, c: (b * nc + c, 0)),
                   pl.BlockSpec((lc, sc_w), lambda b, c: (b * nc + c, 0)),
                   per_seq(mc0), per_seq(h0), per_seq(sc0)],
        out_shape=outs,
        scratch_shapes=[pltpu.VMEM((8 + lc, m_xbc), F32), pltpu.VMEM((8 + lc, sc_w), F32)],
        compiler_params=_params(("parallel", "arbitrary")),
        name="ssd",
    )(proj, proj, dtp, proj, proj, proj, cw, cb, dtb, alog, dfull, nw, scw, mc0, h0, sc0)


def _merge_kernel(x_ref, gl_ref, yr_ref, yb_ref, ym_ref, yc_ref, wb_ref, wo_ref, nw_ref, out_ref):
    d = x_ref.shape[1]
    offs = [0]
    for y_ref in (yr_ref, yb_ref, ym_ref, yc_ref):
        offs.append(offs[-1] + y_ref.shape[1])
    merged = None
    for i, y_ref in enumerate((yr_ref, yb_ref, ym_ref, yc_ref)):
        br = jnp.dot(y_ref[...], wb_ref[offs[i]:offs[i + 1], :], preferred_element_type=F32)
        term = _sigmoid(gl_ref[:, i * d:(i + 1) * d]) * br
        merged = term if merged is None else merged + term
    o = jnp.dot(merged.astype(BF16), wo_ref[...], preferred_element_type=F32)
    out_ref[...] = x_ref[...] + _rms(o, nw_ref[...])


def _merge(x, proj, yr, yb, ym, yc, wb, wo, nw):
    m, d = x.shape
    tm = _pick_tile(m, 256, SAMPLE_CHUNK)

    def rows(a):
        return pl.BlockSpec((tm, a.shape[1]), lambda i: (i, 0))

    def full(a):
        return pl.BlockSpec(a.shape, lambda i: (0, 0))

    return pl.pallas_call(
        _merge_kernel,
        grid=(m // tm,),
        in_specs=[rows(x), pl.BlockSpec((tm, N_BRANCH * d), lambda i: (i, COL_GL // (N_BRANCH * d))),
                  rows(yr), rows(yb), rows(ym), rows(yc), full(wb), full(wo), full(nw)],
        out_specs=rows(x),
        out_shape=jax.ShapeDtypeStruct((m, d), F32),
        compiler_params=_params(("parallel",)),
        name="merge",
    )(x, proj, yr, yb, ym, yc, wb, wo, nw)


def _mlp_kernel(x_ref, nw_pre_ref, nw_post_ref, wu_ref, wd_ref, out_ref, h_scr, acc):
    j = pl.program_id(1)

    @pl.when(j == 0)
    def _():
        h_scr[...] = _rms(x_ref[...], nw_pre_ref[...]).astype(BF16)
        acc[...] = jnp.zeros_like(acc)

    f = jnp.dot(h_scr[...], wu_ref[...], preferred_element_type=F32)
    f = jnp.square(jnp.maximum(f, 0.0)).astype(BF16)
    acc[...] += jnp.dot(f, wd_ref[...], preferred_element_type=F32)

    @pl.when(j == pl.num_programs(1) - 1)
    def _():
        out_ref[...] = x_ref[...] + _rms(acc[...], nw_post_ref[...])


def _mlp(x, nw_pre, nw_post, wu, wd):
    m, d = x.shape
    d_ff = wu.shape[1]
    tm = _pick_tile(m, 1280, SAMPLE_CHUNK)
    tf = 512
    return pl.pallas_call(
        _mlp_kernel,
        grid=(m // tm, d_ff // tf),
        in_specs=[pl.BlockSpec((tm, d), lambda i, j: (i, 0)),
                  pl.BlockSpec((1, d), lambda i, j: (0, 0)),
                  pl.BlockSpec((1, d), lambda i, j: (0, 0)),
                  pl.BlockSpec((d, tf), lambda i, j: (0, j)),
                  pl.BlockSpec((tf, d), lambda i, j: (j, 0))],
        out_specs=pl.BlockSpec((tm, d), lambda i, j: (i, 0)),
        out_shape=jax.ShapeDtypeStruct((m, d), F32),
        scratch_shapes=[pltpu.VMEM((tm, d), BF16), pltpu.VMEM((tm, d), F32)],
        compiler_params=_params(("parallel", "arbitrary")),
        name="mlp",
    )(x, nw_pre, nw_post, wu, wd)


def _reorder_w_in(w_in_l, d_model):
    sizes = (512, 512, 512, 512, 512, 512, 512, d_model, d_model + 2 * M_GROUPS * M_STATE,
             d_model // M_HEADDIM, d_model // 2, d_model // 2, d_model // 2, N_BRANCH * d_model)
    parts, off = [], 0
    for s in sizes:
        parts.append(w_in_l[:, off:off + s])
        off += s
    rq, rk, rv, rg, sq, sk, sv, mz, xbc, dt, gb, gc, gh, gl = parts
    w_main = jnp.concatenate([gl, xbc, mz, rq, rk, rv, rg, sq, sk, sv, gb, gc, gh], axis=1).astype(BF16)
    w_dt = jnp.pad(dt, ((0, 0), (0, LANES - dt.shape[1]))).astype(BF16)
    return w_main, w_dt


def _rope_tables(nc, lc, nv, pos0):
    r = jnp.arange(nc * lc)
    c, i = r // lc, r % lc
    pos = jnp.where(c == 0, i, nv + (c - 1) * lc + i) + pos0
    half = RET_DK // 2
    inv = ROPE_BASE ** (-jnp.arange(half, dtype=F32) / half)
    ang = pos.astype(F32)[:, None] * inv[None, :]
    cos, sin = jnp.cos(ang), jnp.sin(ang)
    return jnp.concatenate([cos, cos], axis=1), jnp.concatenate([-sin, sin], axis=1)


def _pad_lanes(v):
    return jnp.pad(v.astype(F32), (0, LANES - v.shape[0]))[None, :]


def _layer(x, l, nb, nc, lc, nv, pos0, sb_fn, states, wts):
    (w_in, w_branch, w_o, w_up, w_down, norm_w, ret_gn_w, m_conv_w, m_conv_b, m_dt_bias, m_a_log,
     m_d, m_norm_w, sc_w, sb_bias) = wts
    s_ret, s_ssm, s_mconv, s_sconv = states
    d = x.shape[1]
    w_main, w_dt = _reorder_w_in(w_in[l], d)
    proj, dtp = _inproj(x, norm_w[l, 0][None, :], w_main, w_dt)
    cos2, sin2 = _rope_tables(nc, lc, nv, pos0)
    y_ret, o_ret = _retention(proj, cos2, sin2, ret_gn_w[l][None, :], s_ret, nb, nc, lc, nv)
    y_sb = sb_fn(proj, sb_bias[l].astype(F32))
    dfull = jnp.repeat(m_d[l].astype(F32), M_HEADDIM)[None, :]
    y_ssm, y_sc, o_mconv, o_ssm, o_sconv = _ssd(
        proj, dtp, m_conv_w[l], m_conv_b[l][None, :], _pad_lanes(m_dt_bias[l]), _pad_lanes(m_a_log[l]),
        dfull, m_norm_w[l][None, :], sc_w[l], s_mconv, s_ssm, s_sconv, nb, nc, lc, nv)
    x = _merge(x, proj, y_ret, y_sb, y_ssm, y_sc, w_branch[l].astype(BF16), w_o[l].astype(BF16),
               norm_w[l, 1][None, :])
    x = _mlp(x, norm_w[l, 2][None, :], norm_w[l, 3][None, :], w_up[l].astype(BF16), w_down[l].astype(BF16))
    return x, proj, (o_ret, o_ssm, o_mconv, o_sconv)


def _valid_rows(a3, nv, lc):
    if a3.shape[1] == lc:
        return a3[:, :nv]
    return jnp.concatenate([a3[:, :nv], a3[:, lc:]], axis=1)


def kernel(x_prompt, x_sample, cache_sb_k, cache_sb_v, state_ret, state_ssm, state_mconv, state_sconv,
           page_table, meta, w_in, w_branch, w_o, w_up, w_down, norm_w, ret_gn_w, m_conv_w, m_conv_b,
           m_dt_bias, m_a_log, m_d, m_norm_w, sc_w, sb_bias):
    nb_p, seq, d = x_prompt.shape
    nb_s, dec_len, _ = x_sample.shape
    depth = w_in.shape[0]
    assert seq % CHUNK == 0 and dec_len == 1 and meta.shape[0] == N_META
    past = page_table.shape[1] * cache_sb_k.shape[2]
    wts = (w_in, w_branch, w_o, w_up, w_down, norm_w, ret_gn_w, m_conv_w, m_conv_b, m_dt_bias, m_a_log,
           m_d, m_norm_w, sc_w, sb_bias)

    nc_p = 1 + seq // CHUNK
    xp = jnp.concatenate([jnp.broadcast_to(meta.astype(F32)[None], (nb_p, N_META, d)),
                          jnp.zeros((nb_p, CHUNK - N_META, d), F32), x_prompt], axis=1)
    xp = xp.reshape(nb_p * nc_p * CHUNK, d)
    lc_s = SAMPLE_CHUNK
    xs = jnp.pad(x_sample, ((0, 0), (0, lc_s - dec_len), (0, 0))).reshape(nb_s * lc_s, d)

    n_phys, page = cache_sb_k.shape[1], cache_sb_k.shape[2]
    ck = cache_sb_k.reshape(depth, n_phys, page * SB_HEADS, SB_DH)
    cv = cache_sb_v.reshape(depth, n_phys, page * SB_HEADS, SB_DH)

    zeros_p = (jnp.zeros((nb_p,) + state_ret.shape[2:], F32), jnp.zeros((nb_p,) + state_ssm.shape[2:], F32),
               jnp.zeros((nb_p,) + state_mconv.shape[2:], F32), jnp.zeros((nb_p,) + state_sconv.shape[2:], F32))
    p_out = [[] for _ in range(6)]
    s_out = [[] for _ in range(6)]
    for l in range(depth):
        xp, proj_p, st_p = _layer(
            xp, l, nb_p, nc_p, CHUNK, N_META, 0,
            lambda pr, bias: _sb_prompt(pr, bias, nb_p, nc_p, N_META), zeros_p, wts)
        xs, proj_s, st_s = _layer(
            xs, l, nb_s, 1, lc_s, dec_len, past,
            lambda pr, bias, l=l: _sb_sample(pr, bias, ck, cv, page_table, l, nb_s, lc_s, dec_len),
            (state_ret[l], state_ssm[l], state_mconv[l], state_sconv[l]), wts)
        for outs, proj, st, nb, nc, lc, nv in ((p_out, proj_p, st_p, nb_p, nc_p, CHUNK, N_META),
                                                (s_out, proj_s, st_s, nb_s, 1, lc_s, dec_len)):
            p3 = proj.reshape(nb, nc * lc, PROJ_W)
            width = SB_HEADS * SB_DH
            for i, off in enumerate((COL_SK, COL_SV)):
                kv = _valid_rows(p3[:, :, off:off + width], nv, lc)
                outs[i].append(kv.reshape(nb, kv.shape[1], SB_HEADS, SB_DH))
            for i in range(4):
                outs[2 + i].append(st[i])
    p_sb_k, p_sb_v, p_ret, p_ssm, p_mconv, p_sconv = [jnp.stack(a) for a in p_out]
    s_sb_k, s_sb_v, s_ret, s_ssm, s_mconv, s_sconv = [jnp.stack(a) for a in s_out]
    y_prompt = xp.reshape(nb_p, nc_p * CHUNK, d)[:, CHUNK:]
    y_sample = xs.reshape(nb_s, lc_s, d)[:, :dec_len]
    return (y_prompt, y_sample, p_sb_k, p_sb_v, p_ret, p_ssm, p_mconv, p_sconv,
            s_sb_k, s_sb_v, s_ret, s_ssm, s_mconv, s_sconv)
```

```python
import functools
import math

import jax
import jax.numpy as jnp
from jax import lax
from jax.experimental import pallas as pl
from jax.experimental.pallas import tpu as pltpu

F32 = jnp.float32
BF16 = jnp.bfloat16
HIGHEST = lax.Precision.HIGHEST

N_META = 16
CHUNK = 128
EPS = 1e-6
RET_HEADS = 4
RET_DK = 128
ROPE_BASE = 10000.0
SB_HEADS = 4
SB_DH = 128
M_HEADDIM = 64
M_GROUPS = 4
M_STATE = 128
M_CONV = 4
SC_CONV = 3
N_BRANCH = 4
SAMPLE_CHUNK = 16
LANES = 128
VMEM_LIMIT = 56 * 1024 * 1024

COL_GL, COL_XBC, COL_MZ = 0, 4096, 6144
COL_RQ, COL_RK, COL_RV, COL_RG = 7168, 7680, 8192, 8704
COL_SQ, COL_SK, COL_SV = 9216, 9728, 10240
COL_GB, COL_GC, COL_GH = 10752, 11264, 11776
PROJ_W = 12288

NT = (((1,), (1,)), ((), ()))
TN = (((0,), (0,)), ((), ()))


def _pick_tile(m, cap, mult):
    best = None
    for t in range(mult, min(m, cap) + 1, mult):
        if m % t == 0:
            best = t
    assert best is not None, (m, cap, mult)
    return best


def _sigmoid(x):
    return 1.0 / (1.0 + jnp.exp(-x))


def _softplus(x):
    return jnp.maximum(x, 0.0) + jnp.log1p(jnp.exp(-jnp.abs(x)))


def _rms(x, w):
    return x * lax.rsqrt(jnp.mean(x * x, axis=-1, keepdims=True) + EPS) * w


def _params(sem):
    return pltpu.CompilerParams(dimension_semantics=sem, vmem_limit_bytes=VMEM_LIMIT)


def _inproj_kernel(x_ref, nw_ref, w_ref, wdt_ref, out_ref, dt_ref, h_scr):
    @pl.when(pl.program_id(1) == 0)
    def _():
        hb = _rms(x_ref[...], nw_ref[...]).astype(BF16)
        h_scr[...] = hb
        dt_ref[...] = jnp.dot(hb, wdt_ref[...], preferred_element_type=F32)

    out_ref[...] = jnp.dot(h_scr[...], w_ref[...], preferred_element_type=F32)


def _inproj(x, nw, w_main, w_dt):
    m, d = x.shape
    tm = _pick_tile(m, 1280, SAMPLE_CHUNK)
    tn = 1024
    return pl.pallas_call(
        _inproj_kernel,
        grid=(m // tm, PROJ_W // tn),
        in_specs=[
            pl.BlockSpec((tm, d), lambda i, j: (i, 0)),
            pl.BlockSpec((1, d), lambda i, j: (0, 0)),
            pl.BlockSpec((d, tn), lambda i, j: (0, j)),
            pl.BlockSpec((d, LANES), lambda i, j: (0, 0)),
        ],
        out_specs=[
            pl.BlockSpec((tm, tn), lambda i, j: (i, j)),
            pl.BlockSpec((tm, LANES), lambda i, j: (i, 0)),
        ],
        out_shape=[jax.ShapeDtypeStruct((m, PROJ_W), F32), jax.ShapeDtypeStruct((m, LANES), F32)],
        scratch_shapes=[pltpu.VMEM((tm, d), BF16)],
        compiler_params=_params(("parallel", "arbitrary")),
        name="inproj",
    )(x, nw, w_main, w_dt)


def _ret_kernel(nv, lc, q_ref, k_ref, v_ref, g_ref, cos_ref, sin_ref, gnw_ref, s0_ref, y_ref, s_ref):
    c = pl.program_id(1)

    @pl.when(c == 0)
    def _():
        s_ref[...] = s0_ref[...]

    row = lax.broadcasted_iota(jnp.int32, (lc, LANES), 0)
    rowf = row.astype(F32)
    valid = jnp.logical_or(c > 0, row < nv)
    n_valid = jnp.where(c == 0, nv, lc).astype(F32)
    l_eff = jnp.full((lc, LANES), 1.0, F32) * n_valid
    l_eff_sq = jnp.full((LANES, LANES), 1.0, F32) * n_valid
    diff = (lax.broadcasted_iota(jnp.int32, (lc, lc), 0)
            - lax.broadcasted_iota(jnp.int32, (lc, lc), 1)).astype(F32)
    cos = cos_ref[...]
    sin = sin_ref[...]
    for h in range(RET_HEADS):
        lg = math.log1p(-(2.0 ** (-5 - h)))
        sl = slice(h * LANES, (h + 1) * LANES)
        q = jnp.where(valid, q_ref[:, sl], 0.0)
        k = jnp.where(valid, k_ref[:, sl], 0.0)
        v = jnp.where(valid, v_ref[:, sl], 0.0).astype(BF16)
        qr = ((q * cos + pltpu.roll(q, 64, 1) * sin) * (RET_DK ** -0.5)).astype(BF16)
        kr = k * cos + pltpu.roll(k, 64, 1) * sin
        decay = jnp.where(diff >= 0, jnp.exp(lg * jnp.maximum(diff, 0.0)), 0.0)
        scores = lax.dot_general(qr, kr.astype(BF16), NT, preferred_element_type=F32) * decay
        s = s_ref[h]
        o = jnp.dot(scores.astype(BF16), v, preferred_element_type=F32)
        o = o + jnp.dot(qr, s.astype(BF16), preferred_element_type=F32) * jnp.exp(lg * (rowf + 1.0))
        kd = (kr * jnp.exp(lg * (l_eff - 1.0 - rowf))).astype(BF16)
        s_ref[h] = jnp.exp(lg * l_eff_sq) * s + lax.dot_general(kd, v, TN, preferred_element_type=F32)
        mu = jnp.mean(o, axis=-1, keepdims=True)
        d = o - mu
        var = jnp.mean(d * d, axis=-1, keepdims=True)
        g = g_ref[:, sl]
        y = d * lax.rsqrt(var + EPS) * gnw_ref[:, sl] * (g * _sigmoid(g))
        y_ref[:, sl] = y.astype(BF16)


def _retention(proj, cos2, sin2, gnw, s0, nb, nc, lc, nv):
    m = proj.shape[0]
    w = RET_HEADS * LANES

    def col(off):
        return pl.BlockSpec((lc, w), lambda b, c: (b * nc + c, off // w))

    return pl.pallas_call(
        functools.partial(_ret_kernel, nv, lc),
        grid=(nb, nc),
        in_specs=[
            col(COL_RQ), col(COL_RK), col(COL_RV), col(COL_RG),
            pl.BlockSpec((lc, LANES), lambda b, c: (c, 0)),
            pl.BlockSpec((lc, LANES), lambda b, c: (c, 0)),
            pl.BlockSpec((1, w), lambda b, c: (0, 0)),
            pl.BlockSpec((None, RET_HEADS, LANES, LANES), lambda b, c: (b, 0, 0, 0)),
        ],
        out_specs=[
            pl.BlockSpec((lc, w), lambda b, c: (b * nc + c, 0)),
            pl.BlockSpec((None, RET_HEADS, LANES, LANES), lambda b, c: (b, 0, 0, 0)),
        ],
        out_shape=[jax.ShapeDtypeStruct((m, w), BF16),
                   jax.ShapeDtypeStruct((nb, RET_HEADS, LANES, LANES), F32)],
        compiler_params=_params(("parallel", "arbitrary")),
        name="retention",
    )(proj, proj, proj, proj, cos2, sin2, gnw, s0)


def _cumsum_matrix(n):
    j = lax.broadcasted_iota(jnp.int32, (n, n + LANES), 0)
    s = lax.broadcasted_iota(jnp.int32, (n, n + LANES), 1)
    return jnp.where(jnp.logical_or(j > s, s >= n), 1.0, 0.0).astype(BF16)


def _neg_abs(x):
    bits = lax.bitcast_convert_type(x, jnp.uint32) | jnp.uint32(0x80000000)
    return lax.bitcast_convert_type(bits, F32)


def _stay_cost(z):
    return jnp.maximum(z, 0.0) + jnp.log(1.0 + jnp.exp(_neg_abs(z)))


def _split_dot(x, w):
    hi = x.astype(BF16)
    lo = (x - hi.astype(F32)).astype(BF16)
    return (jnp.dot(hi, w, preferred_element_type=F32)
            + jnp.dot(lo, w, preferred_element_type=F32))


def _sbp_kernel(nv, qc, bias_ref, q_ref, k_ref, v_ref, o_ref, qb_ref, acc_ref, carry_ref):
    h = pl.program_id(1)
    s = pl.program_id(2)
    bias = bias_ref[h]
    scale = SB_DH ** -0.5
    rows = qc * CHUNK
    u1 = _cumsum_matrix(CHUNK)
    u2 = _cumsum_matrix(2 * CHUNK)

    def one_block(r0, nr, j, mask):
        start = j * CHUNK if isinstance(j, int) else pl.multiple_of(j * CHUNK, CHUNK)
        kb = k_ref[pl.ds(start, CHUNK), :].astype(BF16)
        vb = v_ref[pl.ds(start, CHUNK), :].astype(BF16)
        z = lax.dot_general(qb_ref[r0:r0 + nr, :], kb, NT, preferred_element_type=F32) + bias
        cost = _stay_cost(z)
        r = jnp.dot(jnp.where(mask, cost, 0.0).astype(BF16), u1, preferred_element_type=F32)
        c = carry_ref[r0:r0 + nr, :]
        a = jnp.where(mask, jnp.exp(z - (cost + r[:, :CHUNK] + c)), 0.0)
        acc_ref[r0:r0 + nr, :] += jnp.dot(a.astype(BF16), vb, preferred_element_type=F32)
        carry_ref[r0:r0 + nr, :] = c + r[:, CHUNK:]

    def two_blocks(j_hi):
        start = pl.multiple_of((j_hi - 1) * CHUNK, CHUNK)
        kb = k_ref[pl.ds(start, 2 * CHUNK), :].astype(BF16)
        vb = v_ref[pl.ds(start, 2 * CHUNK), :].astype(BF16)
        z = lax.dot_general(qb_ref[...], kb, NT, preferred_element_type=F32) + bias
        cost = _stay_cost(z)
        r = jnp.dot(cost.astype(BF16), u2, preferred_element_type=F32)
        c = carry_ref[...]
        a = jnp.exp(z - (cost + r[:, :2 * CHUNK] + jnp.concatenate([c, c], axis=1)))
        acc_ref[...] += jnp.dot(a.astype(BF16), vb, preferred_element_type=F32)
        carry_ref[...] = c + r[:, 2 * CHUNK:]

    ci = lax.broadcasted_iota(jnp.int32, (CHUNK, CHUNK), 1)
    ri = lax.broadcasted_iota(jnp.int32, (CHUNK, CHUNK), 0)

    @pl.when(s == 0)
    def _():
        qb_ref[0:CHUNK, :] = (q_ref[0:CHUNK, :] * scale).astype(BF16)
        acc_ref[0:CHUNK, :] = jnp.zeros((CHUNK, SB_DH), F32)
        carry_ref[0:CHUNK, :] = jnp.zeros((CHUNK, CHUNK), F32)
        one_block(0, CHUNK, 0, jnp.logical_and(ci < ri, ci < nv))
        o_ref[0:CHUNK, :] = acc_ref[0:CHUNK, :].astype(BF16)

    @pl.when(s > 0)
    def _():
        jb = 1 + (s - 1) * qc
        q0 = pl.multiple_of(jb * CHUNK, CHUNK)
        qb_ref[...] = (q_ref[pl.ds(q0, rows), :] * scale).astype(BF16)
        acc_ref[...] = jnp.zeros((rows, SB_DH), F32)
        carry_ref[...] = jnp.zeros((rows, CHUNK), F32)
        for m in range(qc - 1, -1, -1):
            nr = rows - m * CHUNK
            cm = lax.broadcasted_iota(jnp.int32, (nr, CHUNK), 1)
            rm = lax.broadcasted_iota(jnp.int32, (nr, CHUNK), 0)
            one_block(m * CHUNK, nr, jb + m, cm < rm)

        def body(t, carry):
            two_blocks(jb - 1 - 2 * t)
            return carry

        lax.fori_loop(0, ((s - 1) * qc) // 2, body, 0)
        ca = lax.broadcasted_iota(jnp.int32, (rows, CHUNK), 1)
        one_block(0, rows, 0, ca < nv)
        o_ref[pl.ds(q0, rows), :] = acc_ref[...].astype(BF16)


def _sb_prompt(proj, bias, nb, nc, nv):
    m = proj.shape[0]
    t_pad = nc * CHUNK
    qc = 4 if (nc - 1) % 4 == 0 else 2
    assert (nc - 1) % qc == 0
    rows = qc * CHUNK

    def seq(off):
        return pl.BlockSpec((t_pad, SB_DH), lambda b, h, s: (b, off // SB_DH + h))

    return pl.pallas_call(
        functools.partial(_sbp_kernel, nv, qc),
        grid=(nb, SB_HEADS, 1 + (nc - 1) // qc),
        in_specs=[pl.BlockSpec(memory_space=pltpu.SMEM), seq(COL_SQ), seq(COL_SK), seq(COL_SV)],
        out_specs=pl.BlockSpec((t_pad, SB_DH), lambda b, h, s: (b, h)),
        out_shape=jax.ShapeDtypeStruct((m, SB_HEADS * SB_DH), BF16),
        scratch_shapes=[pltpu.VMEM((rows, SB_DH), BF16), pltpu.VMEM((rows, SB_DH), F32),
                        pltpu.VMEM((rows, CHUNK), F32)],
        compiler_params=_params(("parallel", "parallel", "arbitrary")),
        name="sb_prompt",
    )(bias, proj, proj, proj)


SBS_PAGES = 8
HEAD_BITS = SB_HEADS.bit_length() - 1
assert 1 << HEAD_BITS == SB_HEADS


def _head_of(x):
    return jnp.bitwise_and(x, SB_HEADS - 1)


def _sbs_constants(page_w):
    j = jnp.arange(page_w)[:, None]
    t = jnp.arange(2 * page_w)[None, :]
    u = (j % SB_HEADS == t % SB_HEADS) & ((j > t) | (t >= page_w))
    rows = SB_HEADS * SBS_PAGES
    a = jnp.arange(2 * rows)[:, None]
    b = jnp.arange(rows)[None, :]
    mix = (a % SB_HEADS == b % SB_HEADS) & ((a >= rows) | (b // SB_HEADS < a // SB_HEADS))
    return u.astype(BF16), mix.astype(F32)


def _sbs_kernel(lc, n_steps, past, dec_len, pt_ref, bias_ref, q_ref, kn_ref, vn_ref, u_ref, mix_ref, *refs):
    del pt_ref
    pp = SBS_PAGES
    k_refs, v_refs = refs[:pp], refs[pp:2 * pp]
    o_ref, acc_ref, carry_ref = refs[2 * pp:]
    s = pl.program_id(1)
    rows = SB_HEADS * pp
    page_w = k_refs[0].shape[0]
    scale = SB_DH ** -0.5
    rr = lax.broadcasted_iota(jnp.int32, (rows, page_w), 0)
    nn = lax.broadcasted_iota(jnp.int32, (rows, page_w), 1)
    r8 = lax.broadcasted_iota(jnp.int32, (8, SB_DH), 0)

    @pl.when(s == 0)
    def _():
        kpos = past + lax.broadcasted_iota(jnp.int32, (1, 1), 1)
        visible = kpos < past + dec_len - 1
        acc0 = jnp.zeros((8, SB_DH), F32)
        carry0 = jnp.zeros((rows, page_w), F32)
        for h in range(SB_HEADS):
            sl = slice(h * SB_DH, (h + 1) * SB_DH)
            z_new = (jnp.sum(q_ref[0:1, sl] * kn_ref[0:1, sl], axis=1, keepdims=True) * scale
                     + bias_ref[h])
            ls_new = -_softplus(z_new)
            a_new = jnp.where(visible, jnp.exp(z_new + ls_new), 0.0)
            acc0 = jnp.where(r8 == h, a_new * vn_ref[0:1, sl], acc0)
            carry0 = jnp.where(_head_of(rr) == h, jnp.where(visible, ls_new, 0.0), carry0)
        acc_ref[...] = acc0
        carry_ref[...] = carry0

    r_q = lax.broadcasted_iota(jnp.int32, (rows, SB_DH), 0)
    qrep = jnp.zeros((rows, SB_DH), F32)
    for h in range(SB_HEADS):
        qrep = jnp.where(_head_of(r_q) == h, q_ref[0:1, h * SB_DH:(h + 1) * SB_DH], qrep)

    z = jnp.zeros((rows, page_w), F32)
    for i in range(pp):
        qsel = jnp.where(lax.shift_right_logical(r_q, HEAD_BITS) == i, qrep, 0.0).astype(BF16)
        z = z + lax.dot_general(qsel, k_refs[i][...].astype(BF16), NT, preferred_element_type=F32)

    own = _head_of(nn) == _head_of(rr)
    bias = jnp.zeros((rows, page_w), F32)
    for h in range(SB_HEADS):
        bias = jnp.where(_head_of(rr) == h, bias_ref[h], bias)
    z = z * scale + bias
    ls = jnp.where(own, -_softplus(z), 0.0)
    r = _split_dot(ls, u_ref[...])
    rest, tot = r[:, :page_w], r[:, page_w:]
    pt = jnp.dot(mix_ref[...], tot, precision=HIGHEST, preferred_element_type=F32)
    carry = carry_ref[...]
    a = jnp.where(own, jnp.exp(z + ls + rest + pt[:rows] + carry), 0.0).astype(BF16)
    carry_ref[...] = carry + pt[rows:]

    acc = acc_ref[...]
    for g in range(pp // 2):
        a8 = a[8 * g:8 * g + 8]
        o0 = jnp.dot(a8, v_refs[2 * g][...].astype(BF16), preferred_element_type=F32)
        o1 = jnp.dot(a8, v_refs[2 * g + 1][...].astype(BF16), preferred_element_type=F32)
        acc = acc + jnp.where(r8 < SB_HEADS, o0, o1)
    acc_ref[...] = acc

    @pl.when(s == n_steps - 1)
    def _():
        tot8 = acc_ref[...]
        o4 = tot8[0:SB_HEADS] + tot8[SB_HEADS:2 * SB_HEADS]
        row0 = lax.broadcasted_iota(jnp.int32, (lc, SB_DH), 0) == 0
        for h in range(SB_HEADS):
            o_ref[:, h * SB_DH:(h + 1) * SB_DH] = jnp.where(row0, o4[h:h + 1], 0.0).astype(BF16)


def _sb_sample(proj, bias, cache_k, cache_v, page_table, layer, nb, lc, dec_len):
    m = proj.shape[0]
    n_pages = page_table.shape[1]
    pp = SBS_PAGES
    assert n_pages % pp == 0 and pp % 2 == 0 and SB_HEADS * 2 == 8
    n_steps = n_pages // pp
    page_rows = cache_k.shape[2]
    past = n_pages * (page_rows // SB_HEADS)
    width = SB_HEADS * SB_DH
    u, mix = _sbs_constants(page_rows)

    def page_spec(i):
        return pl.BlockSpec((None, None, page_rows, SB_DH),
                            lambda b, s, pt: (layer, pt[b, n_pages - 1 - (s * pp + i)], 0, 0))

    def col(off):
        return pl.BlockSpec((lc, width), lambda b, s, pt: (b, off // width))

    grid_spec = pltpu.PrefetchScalarGridSpec(
        num_scalar_prefetch=1,
        grid=(nb, n_steps),
        in_specs=[pl.BlockSpec(memory_space=pltpu.SMEM), col(COL_SQ), col(COL_SK), col(COL_SV),
                  pl.BlockSpec(u.shape, lambda b, s, pt: (0, 0)),
                  pl.BlockSpec(mix.shape, lambda b, s, pt: (0, 0))]
                 + [page_spec(i) for i in range(pp)] + [page_spec(i) for i in range(pp)],
        out_specs=pl.BlockSpec((lc, width), lambda b, s, pt: (b, 0)),
        scratch_shapes=[pltpu.VMEM((8, SB_DH), F32), pltpu.VMEM((SB_HEADS * pp, page_rows), F32)],
    )
    return pl.pallas_call(
        functools.partial(_sbs_kernel, lc, n_steps, past, dec_len),
        grid_spec=grid_spec,
        out_shape=jax.ShapeDtypeStruct((m, width), BF16),
        compiler_params=_params(("parallel", "arbitrary")),
        name="sb_sample",
    )(page_table, bias, proj, proj, proj, u, mix, *([cache_k] * pp), *([cache_v] * pp))


def _conv_taps(buf, w_ref, lc, taps):
    base = 8 - (taps - 1)
    out = buf[base:base + lc, :] * w_ref[0:1, :]
    for k in range(1, taps):
        out = out + buf[base + k:base + k + lc, :] * w_ref[k:k + 1, :]
    return out


def _roll_tail(buf, c, nv, lc, taps):
    base = 8 - (taps - 1)

    @pl.when(c == 0)
    def _():
        buf[base:8, :] = buf[base + nv:8 + nv, :]

    @pl.when(c > 0)
    def _():
        buf[base:8, :] = buf[base + lc:8 + lc, :]


def _ssd_kernel(nv, lc, n_chunks,
                xbc_ref, mz_ref, dt_ref, gb_ref, gc_ref, gh_ref,
                cw_ref, cb_ref, dtb_ref, alog_ref, dfull_ref, nw_ref, scw_ref,
                mc0_ref, h0_ref, sc0_ref,
                yssm_ref, ysc_ref, mc_ref, h_ref, sc_ref,
                cbuf, sbuf):
    c = pl.program_id(1)
    n_heads = h_ref.shape[0]
    m_inner = n_heads * M_HEADDIM
    rep = n_heads // M_GROUPS

    @pl.when(c == 0)
    def _():
        cbuf[8 - (M_CONV - 1):8, :] = mc0_ref[...]
        sbuf[8 - (SC_CONV - 1):8, :] = sc0_ref[...]
        h_ref[...] = h0_ref[...]

    cbuf[8:8 + lc, :] = xbc_ref[...]
    conv = _conv_taps(cbuf, cw_ref, lc, M_CONV) + cb_ref[...]
    _roll_tail(cbuf, c, nv, lc, M_CONV)
    xc = conv * _sigmoid(conv)
    mx = xc[:, :m_inner]
    mb = xc[:, m_inner:m_inner + M_GROUPS * M_STATE].astype(BF16)
    mc = xc[:, m_inner + M_GROUPS * M_STATE:].astype(BF16)

    row = lax.broadcasted_iota(jnp.int32, (lc, LANES), 0)
    valid = jnp.logical_or(c > 0, row < nv)
    dt = jnp.where(valid, _softplus(dt_ref[...] + dtb_ref[...]), 0.0)
    dta = dt * (-jnp.exp(alog_ref[...]))
    ii = lax.broadcasted_iota(jnp.int32, (lc, lc), 0)
    jj = lax.broadcasted_iota(jnp.int32, (lc, lc), 1)
    causal = ii >= jj
    la = jnp.dot(jnp.where(causal, 1.0, 0.0), dta, precision=HIGHEST, preferred_element_type=F32)

    def transposed(a):
        if lc < LANES:
            a = jnp.concatenate([a, jnp.zeros((LANES - lc, LANES), F32)], axis=0)
        return a.T[:, :lc]

    la_t = transposed(la)
    dt_t = transposed(dt)
    la_last = la[lc - 1:lc, :]
    ela = jnp.exp(la)
    w_end = jnp.exp(la_last - la) * dt
    e_last = jnp.exp(la_last)

    ys = []
    for g in range(M_GROUPS):
        bm = mb[:, g * M_STATE:(g + 1) * M_STATE]
        cm = mc[:, g * M_STATE:(g + 1) * M_STATE]
        cb = lax.dot_general(cm, bm, NT, preferred_element_type=F32)
        for r in range(rep):
            hd = g * rep + r
            seg = la[:, hd:hd + 1] - la_t[hd:hd + 1, :]
            dec = jnp.where(causal, jnp.exp(jnp.where(causal, seg, 0.0)), 0.0)
            wts = (cb * dec * dt_t[hd:hd + 1, :]).astype(BF16)
            x_h = mx[:, hd * M_HEADDIM:(hd + 1) * M_HEADDIM]
            hst = h_ref[hd]
            y_h = jnp.dot(wts, x_h.astype(BF16), preferred_element_type=F32)
            y_h = y_h + (lax.dot_general(cm, hst.astype(BF16), NT, preferred_element_type=F32)
                         * ela[:, hd:hd + 1])
            xw = (x_h * w_end[:, hd:hd + 1]).astype(BF16)
            h_ref[hd] = (e_last[:, hd:hd + 1] * hst
                         + lax.dot_general(xw, bm, TN, preferred_element_type=F32))
            ys.append(y_h)
    y = jnp.concatenate(ys, axis=1) + dfull_ref[...] * mx
    mz = mz_ref[...]
    y = y * (mz * _sigmoid(mz))
    yssm_ref[...] = _rms(y, nw_ref[...]).astype(BF16)

    sbuf[8:8 + lc, :] = gc_ref[...] * gh_ref[...]
    ysc_ref[...] = (gb_ref[...] * _conv_taps(sbuf, scw_ref, lc, SC_CONV)).astype(BF16)
    _roll_tail(sbuf, c, nv, lc, SC_CONV)

    @pl.when(c == n_chunks - 1)
    def _():
        mc_ref[...] = cbuf[8 - (M_CONV - 1):8, :]
        sc_ref[...] = sbuf[8 - (SC_CONV - 1):8, :]


def _ssd(proj, dtp, cw, cb, dtb, alog, dfull, nw, scw, mc0, h0, sc0, nb, nc, lc, nv):
    m = proj.shape[0]
    n_heads, m_xbc, sc_w = h0.shape[1], cw.shape[1], scw.shape[1]
    m_inner = n_heads * M_HEADDIM

    def col(off, w):
        return pl.BlockSpec((lc, w), lambda b, c: (b * nc + c, off // w))

    def full(a):
        return pl.BlockSpec(a.shape, lambda b, c: (0,) * a.ndim)

    def per_seq(a):
        return pl.BlockSpec((None,) + a.shape[1:], lambda b, c: (b,) + (0,) * (a.ndim - 1))

    outs = [jax.ShapeDtypeStruct((m, m_inner), BF16), jax.ShapeDtypeStruct((m, sc_w), BF16),
            jax.ShapeDtypeStruct(mc0.shape, F32), jax.ShapeDtypeStruct(h0.shape, F32),
            jax.ShapeDtypeStruct(sc0.shape, F32)]
    return pl.pallas_call(
        functools.partial(_ssd_kernel, nv, lc, nc),
        grid=(nb, nc),
        in_specs=[col(COL_XBC, m_xbc), col(COL_MZ, m_inner),
                  pl.BlockSpec((lc, LANES), lambda b, c: (b * nc + c, 0)),
                  col(COL_GB, sc_w), col(COL_GC, sc_w), col(COL_GH, sc_w),
                  full(cw), full(cb), full(dtb), full(alog), full(dfull), full(nw), full(scw),
                  per_seq(mc0), per_seq(h0), per_seq(sc0)],
        out_specs=[pl.BlockSpec((lc, m_inner), lambda b, c: (b * nc + c, 0)),
                   pl.BlockSpec((lc, sc_w), lambda b, c: (b * nc + c, 0)),
                   per_seq(mc0), per_seq(h0), per_seq(sc0)],
        out_shape=outs,
        scratch_shapes=[pltpu.VMEM((8 + lc, m_xbc), F32), pltpu.VMEM((8 + lc, sc_w), F32)],
        compiler_params=_params(("parallel", "arbitrary")),
        name="ssd",
    )(proj, proj, dtp, proj, proj, proj, cw, cb, dtb, alog, dfull, nw, scw, mc0, h0, sc0)


def _merge_kernel(x_ref, gl_ref, yr_ref, yb_ref, ym_ref, yc_ref, wb_ref, wo_ref, nw_ref, out_ref):
    d = x_ref.shape[1]
    offs = [0]
    for y_ref in (yr_ref, yb_ref, ym_ref, yc_ref):
        offs.append(offs[-1] + y_ref.shape[1])
    merged = None
    for i, y_ref in enumerate((yr_ref, yb_ref, ym_ref, yc_ref)):
        br = jnp.dot(y_ref[...], wb_ref[offs[i]:offs[i + 1], :], preferred_element_type=F32)
        term = _sigmoid(gl_ref[:, i * d:(i + 1) * d]) * br
        merged = term if merged is None else merged + term
    o = jnp.dot(merged.astype(BF16), wo_ref[...], preferred_element_type=F32)
    out_ref[...] = x_ref[...] + _rms(o, nw_ref[...])


def _merge(x, proj, yr, yb, ym, yc, wb, wo, nw):
    m, d = x.shape
    tm = _pick_tile(m, 256, SAMPLE_CHUNK)

    def rows(a):
        return pl.BlockSpec((tm, a.shape[1]), lambda i: (i, 0))

    def full(a):
        return pl.BlockSpec(a.shape, lambda i: (0, 0))

    return pl.pallas_call(
        _merge_kernel,
        grid=(m // tm,),
        in_specs=[rows(x), pl.BlockSpec((tm, N_BRANCH * d), lambda i: (i, COL_GL // (N_BRANCH * d))),
                  rows(yr), rows(yb), rows(ym), rows(yc), full(wb), full(wo), full(nw)],
        out_specs=rows(x),
        out_shape=jax.ShapeDtypeStruct((m, d), F32),
        compiler_params=_params(("parallel",)),
        name="merge",
    )(x, proj, yr, yb, ym, yc, wb, wo, nw)


def _mlp_kernel(x_ref, nw_pre_ref, nw_post_ref, wu_ref, wd_ref, out_ref, h_scr, acc):
    j = pl.program_id(1)

    @pl.when(j == 0)
    def _():
        h_scr[...] = _rms(x_ref[...], nw_pre_ref[...]).astype(BF16)
        acc[...] = jnp.zeros_like(acc)

    f = jnp.dot(h_scr[...], wu_ref[...], preferred_element_type=F32)
    f = jnp.square(jnp.maximum(f, 0.0)).astype(BF16)
    acc[...] += jnp.dot(f, wd_ref[...], preferred_element_type=F32)

    @pl.when(j == pl.num_programs(1) - 1)
    def _():
        out_ref[...] = x_ref[...] + _rms(acc[...], nw_post_ref[...])


def _mlp(x, nw_pre, nw_post, wu, wd):
    m, d = x.shape
    d_ff = wu.shape[1]
    tm = _pick_tile(m, 1280, SAMPLE_CHUNK)
    tf = 512
    return pl.pallas_call(
        _mlp_kernel,
        grid=(m // tm, d_ff // tf),
        in_specs=[pl.BlockSpec((tm, d), lambda i, j: (i, 0)),
                  pl.BlockSpec((1, d), lambda i, j: (0, 0)),
                  pl.BlockSpec((1, d), lambda i, j: (0, 0)),
                  pl.BlockSpec((d, tf), lambda i, j: (0, j)),
                  pl.BlockSpec((tf, d), lambda i, j: (j, 0))],
        out_specs=pl.BlockSpec((tm, d), lambda i, j: (i, 0)),
        out_shape=jax.ShapeDtypeStruct((m, d), F32),
        scratch_shapes=[pltpu.VMEM((tm, d), BF16), pltpu.VMEM((tm, d), F32)],
        compiler_params=_params(("parallel", "arbitrary")),
        name="mlp",
    )(x, nw_pre, nw_post, wu, wd)


def _reorder_w_in(w_in_l, d_model):
    sizes = (512, 512, 512, 512, 512, 512, 512, d_model, d_model + 2 * M_GROUPS * M_STATE,
             d_model // M_HEADDIM, d_model // 2, d_model // 2, d_model // 2, N_BRANCH * d_model)
    parts, off = [], 0
    for s in sizes:
        parts.append(w_in_l[:, off:off + s])
        off += s
    rq, rk, rv, rg, sq, sk, sv, mz, xbc, dt, gb, gc, gh, gl = parts
    w_main = jnp.concatenate([gl, xbc, mz, rq, rk, rv, rg, sq, sk, sv, gb, gc, gh], axis=1).astype(BF16)
    w_dt = jnp.pad(dt, ((0, 0), (0, LANES - dt.shape[1]))).astype(BF16)
    return w_main, w_dt


def _rope_tables(nc, lc, nv, pos0):
    r = jnp.arange(nc * lc)
    c, i = r // lc, r % lc
    pos = jnp.where(c == 0, i, nv + (c - 1) * lc + i) + pos0
    half = RET_DK // 2
    inv = ROPE_BASE ** (-jnp.arange(half, dtype=F32) / half)
    ang = pos.astype(F32)[:, None] * inv[None, :]
    cos, sin = jnp.cos(ang), jnp.sin(ang)
    return jnp.concatenate([cos, cos], axis=1), jnp.concatenate([-sin, sin], axis=1)


def _pad_lanes(v):
    return jnp.pad(v.astype(F32), (0, LANES - v.shape[0]))[None, :]


def _layer(x, l, nb, nc, lc, nv, pos0, sb_fn, states, wts):
    (w_in, w_branch, w_o, w_up, w_down, norm_w, ret_gn_w, m_conv_w, m_conv_b, m_dt_bias, m_a_log,
     m_d, m_norm_w, sc_w, sb_bias) = wts
    s_ret, s_ssm, s_mconv, s_sconv = states
    d = x.shape[1]
    w_main, w_dt = _reorder_w_in(w_in[l], d)
    proj, dtp = _inproj(x, norm_w[l, 0][None, :], w_main, w_dt)
    cos2, sin2 = _rope_tables(nc, lc, nv, pos0)
    y_ret, o_ret = _retention(proj, cos2, sin2, ret_gn_w[l][None, :], s_ret, nb, nc, lc, nv)
    y_sb = sb_fn(proj, sb_bias[l].astype(F32))
    dfull = jnp.repeat(m_d[l].astype(F32), M_HEADDIM)[None, :]
    y_ssm, y_sc, o_mconv, o_ssm, o_sconv = _ssd(
        proj, dtp, m_conv_w[l], m_conv_b[l][None, :], _pad_lanes(m_dt_bias[l]), _pad_lanes(m_a_log[l]),
        dfull, m_norm_w[l][None, :], sc_w[l], s_mconv, s_ssm, s_sconv, nb, nc, lc, nv)
    x = _merge(x, proj, y_ret, y_sb, y_ssm, y_sc, w_branch[l].astype(BF16), w_o[l].astype(BF16),
               norm_w[l, 1][None, :])
    x = _mlp(x, norm_w[l, 2][None, :], norm_w[l, 3][None, :], w_up[l].astype(BF16), w_down[l].astype(BF16))
    return x, proj, (o_ret, o_ssm, o_mconv, o_sconv)


def _valid_rows(a3, nv, lc):
    if a3.shape[1] == lc:
        return a3[:, :nv]
    return jnp.concatenate([a3[:, :nv], a3[:, lc:]], axis=1)


def kernel(x_prompt, x_sample, cache_sb_k, cache_sb_v, state_ret, state_ssm, state_mconv, state_sconv,
           page_table, meta, w_in, w_branch, w_o, w_up, w_down, norm_w, ret_gn_w, m_conv_w, m_conv_b,
           m_dt_bias, m_a_log, m_d, m_norm_w, sc_w, sb_bias):
    nb_p, seq, d = x_prompt.shape
    nb_s, dec_len, _ = x_sample.shape
    depth = w_in.shape[0]
    assert seq % CHUNK == 0 and dec_len == 1 and meta.shape[0] == N_META
    past = page_table.shape[1] * cache_sb_k.shape[2]
    wts = (w_in, w_branch, w_o, w_up, w_down, norm_w, ret_gn_w, m_conv_w, m_conv_b, m_dt_bias, m_a_log,
           m_d, m_norm_w, sc_w, sb_bias)

    nc_p = 1 + seq // CHUNK
    xp = jnp.concatenate([jnp.broadcast_to(meta.astype(F32)[None], (nb_p, N_META, d)),
                          jnp.zeros((nb_p, CHUNK - N_META, d), F32), x_prompt], axis=1)
    xp = xp.reshape(nb_p * nc_p * CHUNK, d)
    lc_s = SAMPLE_CHUNK
    xs = jnp.pad(x_sample, ((0, 0), (0, lc_s - dec_len), (0, 0))).reshape(nb_s * lc_s, d)

    n_phys, page = cache_sb_k.shape[1], cache_sb_k.shape[2]
    ck = cache_sb_k.reshape(depth, n_phys, page * SB_HEADS, SB_DH)
    cv = cache_sb_v.reshape(depth, n_phys, page * SB_HEADS, SB_DH)

    zeros_p = (jnp.zeros((nb_p,) + state_ret.shape[2:], F32), jnp.zeros((nb_p,) + state_ssm.shape[2:], F32),
               jnp.zeros((nb_p,) + state_mconv.shape[2:], F32), jnp.zeros((nb_p,) + state_sconv.shape[2:], F32))
    p_out = [[] for _ in range(6)]
    s_out = [[] for _ in range(6)]
    for l in range(depth):
        xp, proj_p, st_p = _layer(
            xp, l, nb_p, nc_p, CHUNK, N_META, 0,
            lambda pr, bias: _sb_prompt(pr, bias, nb_p, nc_p, N_META), zeros_p, wts)
        xs, proj_s, st_s = _layer(
            xs, l, nb_s, 1, lc_s, dec_len, past,
            lambda pr, bias, l=l: _sb_sample(pr, bias, ck, cv, page_table, l, nb_s, lc_s, dec_len),
            (state_ret[l], state_ssm[l], state_mconv[l], state_sconv[l]), wts)
        for outs, proj, st, nb, nc, lc, nv in ((p_out, proj_p, st_p, nb_p, nc_p, CHUNK, N_META),
                                                (s_out, proj_s, st_s, nb_s, 1, lc_s, dec_len)):
            p3 = proj.reshape(nb, nc * lc, PROJ_W)
            width = SB_HEADS * SB_DH
            for i, off in enumerate((COL_SK, COL_SV)):
                kv = _valid_rows(p3[:, :, off:off + width], nv, lc)
                outs[i].append(kv.reshape(nb, kv.shape[1], SB_HEADS, SB_DH))
            for i in range(4):
                outs[2 + i].append(st[i])
    p_sb_k, p_sb_v, p_ret, p_ssm, p_mconv, p_sconv = [jnp.stack(a) for a in p_out]
    s_sb_k, s_sb_v, s_ret, s_ssm, s_mconv, s_sconv = [jnp.stack(a) for a in s_out]
    y_prompt = xp.reshape(nb_p, nc_p * CHUNK, d)[:, CHUNK:]
    y_sample = xs.reshape(nb_s, lc_s, d)[:, :dec_len]
    return (y_prompt, y_sample, p_sb_k, p_sb_v, p_ret, p_ssm, p_mconv, p_sconv,
            s_sb_k, s_sb_v, s_ret, s_ssm, s_mconv, s_sconv)
```

```python
import functools
import math

import jax
import jax.numpy as jnp
from jax import lax
from jax.experimental import pallas as pl
from jax.experimental.pallas import tpu as pltpu

F32 = jnp.float32
BF16 = jnp.bfloat16
HIGHEST = lax.Precision.HIGHEST

N_META = 16
CHUNK = 128
EPS = 1e-6
RET_HEADS = 4
RET_DK = 128
ROPE_BASE = 10000.0
SB_HEADS = 4
SB_DH = 128
M_HEADDIM = 64
M_GROUPS = 4
M_STATE = 128
M_CONV = 4
SC_CONV = 3
N_BRANCH = 4
SAMPLE_CHUNK = 16
LANES = 128
VMEM_LIMIT = 56 * 1024 * 1024

COL_GL, COL_XBC, COL_MZ = 0, 4096, 6144
COL_RQ, COL_RK, COL_RV, COL_RG = 7168, 7680, 8192, 8704
COL_SQ, COL_SK, COL_SV = 9216, 9728, 10240
COL_GB, COL_GC, COL_GH = 10752, 11264, 11776
PROJ_W = 12288

NT = (((1,), (1,)), ((), ()))
TN = (((0,), (0,)), ((), ()))


def _pick_tile(m, cap, mult):
    best = None
    for t in range(mult, min(m, cap) + 1, mult):
        if m % t == 0:
            best = t
    assert best is not None, (m, cap, mult)
    return best


def _sigmoid(x):
    return 1.0 / (1.0 + jnp.exp(-x))


def _softplus(x):
    return jnp.maximum(x, 0.0) + jnp.log1p(jnp.exp(-jnp.abs(x)))


def _rms(x, w):
    return x * lax.rsqrt(jnp.mean(x * x, axis=-1, keepdims=True) + EPS) * w


def _params(sem):
    return pltpu.CompilerParams(dimension_semantics=sem, vmem_limit_bytes=VMEM_LIMIT)


def _inproj_kernel(x_ref, nw_ref, w_ref, wdt_ref, out_ref, dt_ref, h_scr):
    @pl.when(pl.program_id(1) == 0)
    def _():
        hb = _rms(x_ref[...], nw_ref[...]).astype(BF16)
        h_scr[...] = hb
        dt_ref[...] = jnp.dot(hb, wdt_ref[...], preferred_element_type=F32)

    out_ref[...] = jnp.dot(h_scr[...], w_ref[...], preferred_element_type=F32)


def _inproj(x, nw, w_main, w_dt):
    m, d = x.shape
    tm = _pick_tile(m, 1280, SAMPLE_CHUNK)
    tn = 1024
    return pl.pallas_call(
        _inproj_kernel,
        grid=(m // tm, PROJ_W // tn),
        in_specs=[
            pl.BlockSpec((tm, d), lambda i, j: (i, 0)),
            pl.BlockSpec((1, d), lambda i, j: (0, 0)),
            pl.BlockSpec((d, tn), lambda i, j: (0, j)),
            pl.BlockSpec((d, LANES), lambda i, j: (0, 0)),
        ],
        out_specs=[
            pl.BlockSpec((tm, tn), lambda i, j: (i, j)),
            pl.BlockSpec((tm, LANES), lambda i, j: (i, 0)),
        ],
        out_shape=[jax.ShapeDtypeStruct((m, PROJ_W), F32), jax.ShapeDtypeStruct((m, LANES), F32)],
        scratch_shapes=[pltpu.VMEM((tm, d), BF16)],
        compiler_params=_params(("parallel", "arbitrary")),
        name="inproj",
    )(x, nw, w_main, w_dt)


def _ret_kernel(nv, lc, q_ref, k_ref, v_ref, g_ref, cos_ref, sin_ref, gnw_ref, s0_ref, y_ref, s_ref):
    c = pl.program_id(1)

    @pl.when(c == 0)
    def _():
        s_ref[...] = s0_ref[...]

    row = lax.broadcasted_iota(jnp.int32, (lc, LANES), 0)
    rowf = row.astype(F32)
    valid = jnp.logical_or(c > 0, row < nv)
    n_valid = jnp.where(c == 0, nv, lc).astype(F32)
    l_eff = jnp.full((lc, LANES), 1.0, F32) * n_valid
    l_eff_sq = jnp.full((LANES, LANES), 1.0, F32) * n_valid
    diff = (lax.broadcasted_iota(jnp.int32, (lc, lc), 0)
            - lax.broadcasted_iota(jnp.int32, (lc, lc), 1)).astype(F32)
    cos = cos_ref[...]
    sin = sin_ref[...]
    for h in range(RET_HEADS):
        lg = math.log1p(-(2.0 ** (-5 - h)))
        sl = slice(h * LANES, (h + 1) * LANES)
        q = jnp.where(valid, q_ref[:, sl], 0.0)
        k = jnp.where(valid, k_ref[:, sl], 0.0)
        v = jnp.where(valid, v_ref[:, sl], 0.0).astype(BF16)
        qr = ((q * cos + pltpu.roll(q, 64, 1) * sin) * (RET_DK ** -0.5)).astype(BF16)
        kr = k * cos + pltpu.roll(k, 64, 1) * sin
        decay = jnp.where(diff >= 0, jnp.exp(lg * jnp.maximum(diff, 0.0)), 0.0)
        scores = lax.dot_general(qr, kr.astype(BF16), NT, preferred_element_type=F32) * decay
        s = s_ref[h]
        o = jnp.dot(scores.astype(BF16), v, preferred_element_type=F32)
        o = o + jnp.dot(qr, s.astype(BF16), preferred_element_type=F32) * jnp.exp(lg * (rowf + 1.0))
        kd = (kr * jnp.exp(lg * (l_eff - 1.0 - rowf))).astype(BF16)
        s_ref[h] = jnp.exp(lg * l_eff_sq) * s + lax.dot_general(kd, v, TN, preferred_element_type=F32)
        mu = jnp.mean(o, axis=-1, keepdims=True)
        d = o - mu
        var = jnp.mean(d * d, axis=-1, keepdims=True)
        g = g_ref[:, sl]
        y = d * lax.rsqrt(var + EPS) * gnw_ref[:, sl] * (g * _sigmoid(g))
        y_ref[:, sl] = y.astype(BF16)


def _retention(proj, cos2, sin2, gnw, s0, nb, nc, lc, nv):
    m = proj.shape[0]
    w = RET_HEADS * LANES

    def col(off):
        return pl.BlockSpec((lc, w), lambda b, c: (b * nc + c, off // w))

    return pl.pallas_call(
        functools.partial(_ret_kernel, nv, lc),
        grid=(nb, nc),
        in_specs=[
            col(COL_RQ), col(COL_RK), col(COL_RV), col(COL_RG),
            pl.BlockSpec((lc, LANES), lambda b, c: (c, 0)),
            pl.BlockSpec((lc, LANES), lambda b, c: (c, 0)),
            pl.BlockSpec((1, w), lambda b, c: (0, 0)),
            pl.BlockSpec((None, RET_HEADS, LANES, LANES), lambda b, c: (b, 0, 0, 0)),
        ],
        out_specs=[
            pl.BlockSpec((lc, w), lambda b, c: (b * nc + c, 0)),
            pl.BlockSpec((None, RET_HEADS, LANES, LANES), lambda b, c: (b, 0, 0, 0)),
        ],
        out_shape=[jax.ShapeDtypeStruct((m, w), BF16),
                   jax.ShapeDtypeStruct((nb, RET_HEADS, LANES, LANES), F32)],
        compiler_params=_params(("parallel", "arbitrary")),
        name="retention",
    )(proj, proj, proj, proj, cos2, sin2, gnw, s0)


def _cumsum_matrix(n):
    j = lax.broadcasted_iota(jnp.int32, (n, n + LANES), 0)
    s = lax.broadcasted_iota(jnp.int32, (n, n + LANES), 1)
    return jnp.where(jnp.logical_or(j > s, s >= n), 1.0, 0.0).astype(BF16)


def _neg_abs(x):
    bits = lax.bitcast_convert_type(x, jnp.uint32) | jnp.uint32(0x80000000)
    return lax.bitcast_convert_type(bits, F32)


LOG2E = 1.0 / math.log(2.0)


def _stay_cost2(w):
    return jnp.maximum(w, 0.0) + jnp.log(1.0 + jnp.exp2(_neg_abs(w))) * LOG2E


def _sbp_kernel(nv, qc, bias_ref, q_ref, k_ref, v_ref, o_ref, qb_ref, acc_ref, carry_ref):
    h = pl.program_id(1)
    s = pl.program_id(2)
    bias = bias_ref[h] * LOG2E
    scale = SB_DH ** -0.5 * LOG2E
    rows = qc * CHUNK
    u1 = _cumsum_matrix(CHUNK)
    u2 = _cumsum_matrix(2 * CHUNK)[:, :2 * CHUNK]

    def one_block(r0, nr, j, mask):
        start = j * CHUNK if isinstance(j, int) else pl.multiple_of(j * CHUNK, CHUNK)
        kb = k_ref[pl.ds(start, CHUNK), :].astype(BF16)
        vb = v_ref[pl.ds(start, CHUNK), :].astype(BF16)
        w = lax.dot_general(qb_ref[r0:r0 + nr, :], kb, NT, preferred_element_type=F32) + bias
        cost = _stay_cost2(w)
        r = jnp.dot(jnp.where(mask, cost, 0.0).astype(BF16), u1, preferred_element_type=F32)
        c = carry_ref[r0:r0 + nr, :]
        a = jnp.where(mask, jnp.exp2(w - (cost + r[:, :CHUNK] + c)), 0.0)
        acc_ref[r0:r0 + nr, :] += jnp.dot(a.astype(BF16), vb, preferred_element_type=F32)
        carry_ref[r0:r0 + nr, :] = c + r[:, CHUNK:]

    def two_blocks(j_hi):
        start = pl.multiple_of((j_hi - 1) * CHUNK, CHUNK)
        kb = k_ref[pl.ds(start, 2 * CHUNK), :].astype(BF16)
        vb = v_ref[pl.ds(start, 2 * CHUNK), :].astype(BF16)
        w = lax.dot_general(qb_ref[...], kb, NT, preferred_element_type=F32) + bias
        cost = _stay_cost2(w)
        r = jnp.dot(cost.astype(BF16), u2, preferred_element_type=F32)
        c = carry_ref[...]
        a = jnp.exp2(w - (cost + r + jnp.concatenate([c, c], axis=1)))
        acc_ref[...] += jnp.dot(a.astype(BF16), vb, preferred_element_type=F32)
        carry_ref[...] = c + (r[:, 0:1] + cost[:, 0:1])

    ci = lax.broadcasted_iota(jnp.int32, (CHUNK, CHUNK), 1)
    ri = lax.broadcasted_iota(jnp.int32, (CHUNK, CHUNK), 0)

    @pl.when(s == 0)
    def _():
        qb_ref[0:CHUNK, :] = (q_ref[0:CHUNK, :] * scale).astype(BF16)
        acc_ref[0:CHUNK, :] = jnp.zeros((CHUNK, SB_DH), F32)
        carry_ref[0:CHUNK, :] = jnp.zeros((CHUNK, CHUNK), F32)
        one_block(0, CHUNK, 0, jnp.logical_and(ci < ri, ci < nv))
        o_ref[0:CHUNK, :] = acc_ref[0:CHUNK, :].astype(BF16)

    @pl.when(s > 0)
    def _():
        jb = 1 + (s - 1) * qc
        q0 = pl.multiple_of(jb * CHUNK, CHUNK)
        qb_ref[...] = (q_ref[pl.ds(q0, rows), :] * scale).astype(BF16)
        acc_ref[...] = jnp.zeros((rows, SB_DH), F32)
        carry_ref[...] = jnp.zeros((rows, CHUNK), F32)
        for m in range(qc - 1, -1, -1):
            nr = rows - m * CHUNK
            cm = lax.broadcasted_iota(jnp.int32, (nr, CHUNK), 1)
            rm = lax.broadcasted_iota(jnp.int32, (nr, CHUNK), 0)
            one_block(m * CHUNK, nr, jb + m, cm < rm)

        pairs = qc // 2

        def body(t, carry):
            for p in range(pairs):
                two_blocks(jb - 1 - 2 * (pairs * t + p))
            return carry

        lax.fori_loop(0, ((s - 1) * qc) // (2 * pairs), body, 0)
        ca = lax.broadcasted_iota(jnp.int32, (rows, CHUNK), 1)
        one_block(0, rows, 0, ca < nv)
        o_ref[pl.ds(q0, rows), :] = acc_ref[...].astype(BF16)


def _sb_prompt(proj, bias, nb, nc, nv):
    m = proj.shape[0]
    t_pad = nc * CHUNK
    qc = next(c for c in (8, 4, 2) if (nc - 1) % c == 0)
    rows = qc * CHUNK

    def seq(off):
        return pl.BlockSpec((t_pad, SB_DH), lambda b, h, s: (b, off // SB_DH + h))

    return pl.pallas_call(
        functools.partial(_sbp_kernel, nv, qc),
        grid=(nb, SB_HEADS, 1 + (nc - 1) // qc),
        in_specs=[pl.BlockSpec(memory_space=pltpu.SMEM), seq(COL_SQ), seq(COL_SK), seq(COL_SV)],
        out_specs=pl.BlockSpec((t_pad, SB_DH), lambda b, h, s: (b, h)),
        out_shape=jax.ShapeDtypeStruct((m, SB_HEADS * SB_DH), BF16),
        scratch_shapes=[pltpu.VMEM((rows, SB_DH), BF16), pltpu.VMEM((rows, SB_DH), F32),
                        pltpu.VMEM((rows, CHUNK), F32)],
        compiler_params=_params(("parallel", "parallel", "arbitrary")),
        name="sb_prompt",
    )(bias, proj, proj, proj)


SBS_PAGES = 16
HEAD_BITS = SB_HEADS.bit_length() - 1
assert 1 << HEAD_BITS == SB_HEADS


def _head_of(x):
    return jnp.bitwise_and(x, SB_HEADS - 1)


def _sbs_kernel(lc, n_steps, past, dec_len, pt_ref, bias_ref, q_ref, kn_ref, vn_ref, u_ref, *refs):
    del pt_ref
    pp = SBS_PAGES
    k_refs, v_refs = refs[:pp], refs[pp:2 * pp]
    o_ref, acc_ref, carry_ref = refs[2 * pp:]
    s = pl.program_id(1)
    rows = SB_HEADS * pp
    page_w = k_refs[0].shape[0]
    scale = SB_DH ** -0.5
    rr = lax.broadcasted_iota(jnp.int32, (rows, page_w), 0)
    nn = lax.broadcasted_iota(jnp.int32, (rows, page_w), 1)
    r8 = lax.broadcasted_iota(jnp.int32, (8, SB_DH), 0)

    @pl.when(s == 0)
    def _():
        kpos = past + lax.broadcasted_iota(jnp.int32, (1, 1), 1)
        visible = kpos < past + dec_len - 1
        acc0 = jnp.zeros((8, SB_DH), F32)
        carry0 = jnp.zeros((8, LANES), F32)
        for h in range(SB_HEADS):
            sl = slice(h * SB_DH, (h + 1) * SB_DH)
            z_new = (jnp.sum(q_ref[0:1, sl] * kn_ref[0:1, sl], axis=1, keepdims=True) * scale
                     + bias_ref[h])
            cost_new = _softplus(z_new)
            a_new = jnp.where(visible, jnp.exp(z_new - cost_new), 0.0)
            acc0 = jnp.where(r8 == h, a_new * vn_ref[0:1, sl], acc0)
            carry0 = jnp.where(_head_of(r8) == h, jnp.where(visible, cost_new, 0.0), carry0)
        acc_ref[...] = acc0
        carry_ref[...] = carry0

    r_q = lax.broadcasted_iota(jnp.int32, (rows, SB_DH), 0)
    qrep = jnp.zeros((rows, SB_DH), F32)
    for h in range(SB_HEADS):
        qrep = jnp.where(_head_of(r_q) == h, q_ref[0:1, h * SB_DH:(h + 1) * SB_DH], qrep)

    z = jnp.zeros((rows, page_w), F32)
    for i in range(pp):
        qsel = jnp.where(lax.shift_right_logical(r_q, HEAD_BITS) == i, qrep, 0.0).astype(BF16)
        z = z + lax.dot_general(qsel, k_refs[i][...].astype(BF16), NT, preferred_element_type=F32)

    own = _head_of(nn) == _head_of(rr)
    bias = jnp.zeros((rows, page_w), F32)
    for h in range(SB_HEADS):
        bias = jnp.where(_head_of(rr) == h, bias_ref[h], bias)
    z = z * scale + bias
    cost = jnp.where(own, _softplus(z), 0.0)
    r = jnp.dot(cost.astype(BF16), u_ref[...], preferred_element_type=F32)
    rest, tot = r[:, :page_w], r[:, page_w:]

    offs = []
    base = carry_ref[...]
    for g in range(pp // 2):
        t8 = tot[8 * g:8 * g + 8]
        swapped = pltpu.roll(t8, SB_HEADS, 0)
        offs.append(base + jnp.where(r8 >= SB_HEADS, swapped, 0.0))
        base = base + t8 + swapped
    carry_ref[...] = base
    off = jnp.concatenate(offs, axis=0)
    off = jnp.concatenate([off] * (page_w // LANES), axis=1)
    a = jnp.where(own, jnp.exp(z - (cost + rest + off)), 0.0).astype(BF16)

    acc = acc_ref[...]
    for g in range(pp // 2):
        a8 = a[8 * g:8 * g + 8]
        o0 = jnp.dot(a8, v_refs[2 * g][...].astype(BF16), preferred_element_type=F32)
        o1 = jnp.dot(a8, v_refs[2 * g + 1][...].astype(BF16), preferred_element_type=F32)
        acc = acc + jnp.where(r8 < SB_HEADS, o0, o1)
    acc_ref[...] = acc

    @pl.when(s == n_steps - 1)
    def _():
        tot8 = acc_ref[...]
        o4 = tot8[0:SB_HEADS] + tot8[SB_HEADS:2 * SB_HEADS]
        row0 = lax.broadcasted_iota(jnp.int32, (lc, SB_DH), 0) == 0
        for h in range(SB_HEADS):
            o_ref[:, h * SB_DH:(h + 1) * SB_DH] = jnp.where(row0, o4[h:h + 1], 0.0).astype(BF16)


def _sb_sample(proj, bias, cache_k, cache_v, page_table, layer, nb, lc, dec_len):
    m = proj.shape[0]
    n_pages = page_table.shape[1]
    pp = SBS_PAGES
    assert n_pages % pp == 0 and pp % 2 == 0 and SB_HEADS * 2 == 8
    n_steps = n_pages // pp
    page_rows = cache_k.shape[2]
    past = n_pages * (page_rows // SB_HEADS)
    width = SB_HEADS * SB_DH
    assert SB_DH == LANES
    u = _cumsum_matrix(page_rows)

    def page_spec(i):
        return pl.BlockSpec((None, None, page_rows, SB_DH),
                            lambda b, s, pt: (layer, pt[b, n_pages - 1 - (s * pp + i)], 0, 0))

    def col(off):
        return pl.BlockSpec((lc, width), lambda b, s, pt: (b, off // width))

    grid_spec = pltpu.PrefetchScalarGridSpec(
        num_scalar_prefetch=1,
        grid=(nb, n_steps),
        in_specs=[pl.BlockSpec(memory_space=pltpu.SMEM), col(COL_SQ), col(COL_SK), col(COL_SV),
                  pl.BlockSpec(u.shape, lambda b, s, pt: (0, 0))]
                 + [page_spec(i) for i in range(pp)] + [page_spec(i) for i in range(pp)],
        out_specs=pl.BlockSpec((lc, width), lambda b, s, pt: (b, 0)),
        scratch_shapes=[pltpu.VMEM((8, SB_DH), F32), pltpu.VMEM((8, LANES), F32)],
    )
    return pl.pallas_call(
        functools.partial(_sbs_kernel, lc, n_steps, past, dec_len),
        grid_spec=grid_spec,
        out_shape=jax.ShapeDtypeStruct((m, width), BF16),
        compiler_params=_params(("parallel", "arbitrary")),
        name="sb_sample",
    )(page_table, bias, proj, proj, proj, u, *([cache_k] * pp), *([cache_v] * pp))


def _conv_taps(buf, w_ref, lc, taps):
    full = buf[0:8 + lc, :]
    out = full * w_ref[0:1, :]
    for k in range(1, taps):
        out = _shift_down_one(out) + full * w_ref[k:k + 1, :]
    return out[8:8 + lc, :]


def _shift_down_one(x):
    n, c = x.shape
    first = lax.broadcasted_iota(jnp.int32, (8, c), 0) == 0
    rots = [pltpu.roll(x[8 * i:8 * i + 8, :], 1, 0) for i in range(n // 8)]
    return jnp.concatenate([jnp.where(first, rots[i - 1], rots[i]) for i in range(n // 8)], axis=0)


def _roll_tail(buf, c, nv, lc, taps):
    base = 8 - (taps - 1)

    @pl.when(c == 0)
    def _():
        buf[base:8, :] = buf[base + nv:8 + nv, :]

    @pl.when(c > 0)
    def _():
        buf[base:8, :] = buf[base + lc:8 + lc, :]


def _ssd_kernel(nv, lc, n_chunks,
                xbc_ref, mz_ref, dt_ref, gb_ref, gc_ref, gh_ref,
                cw_ref, cb_ref, dtb_ref, alog_ref, dfull_ref, nw_ref, scw_ref,
                mc0_ref, h0_ref, sc0_ref,
                yssm_ref, ysc_ref, mc_ref, h_ref, sc_ref,
                cbuf, sbuf):
    c = pl.program_id(1)
    n_heads = h_ref.shape[0]
    m_inner = n_heads * M_HEADDIM
    rep = n_heads // M_GROUPS

    @pl.when(c == 0)
    def _():
        cbuf[0:8, :] = jnp.zeros((8, cbuf.shape[1]), F32)
        sbuf[0:8, :] = jnp.zeros((8, sbuf.shape[1]), F32)
        cbuf[8 - (M_CONV - 1):8, :] = mc0_ref[...]
        sbuf[8 - (SC_CONV - 1):8, :] = sc0_ref[...]
        h_ref[...] = h0_ref[...]

    cbuf[8:8 + lc, :] = xbc_ref[...]
    conv = _conv_taps(cbuf, cw_ref, lc, M_CONV) + cb_ref[...]
    _roll_tail(cbuf, c, nv, lc, M_CONV)
    xc = conv * _sigmoid(conv)
    mx = xc[:, :m_inner]
    mb = xc[:, m_inner:m_inner + M_GROUPS * M_STATE].astype(BF16)
    mc = xc[:, m_inner + M_GROUPS * M_STATE:].astype(BF16)

    row = lax.broadcasted_iota(jnp.int32, (lc, LANES), 0)
    valid = jnp.logical_or(c > 0, row < nv)
    dt = jnp.where(valid, _softplus(dt_ref[...] + dtb_ref[...]), 0.0)
    dta = dt * (-jnp.exp(alog_ref[...]))
    ii = lax.broadcasted_iota(jnp.int32, (lc, lc), 0)
    jj = lax.broadcasted_iota(jnp.int32, (lc, lc), 1)
    causal = ii >= jj
    la = jnp.dot(jnp.where(causal, 1.0, 0.0), dta, precision=HIGHEST, preferred_element_type=F32)

    def transposed(a):
        if lc < LANES:
            a = jnp.concatenate([a, jnp.zeros((LANES - lc, LANES), F32)], axis=0)
        return a.T[:, :lc]

    la_t = transposed(la)
    dt_t = transposed(dt)
    la_last = la[lc - 1:lc, :]
    ela = jnp.exp(la)
    w_end = jnp.exp(la_last - la) * dt
    e_last = jnp.exp(la_last)

    assert 2 * M_HEADDIM == LANES and rep % 2 == 0
    first_lanes = lax.broadcasted_iota(jnp.int32, (lc, LANES), 1) < M_HEADDIM
    first_rows = lax.broadcasted_iota(jnp.int32, (LANES, M_STATE), 0) < M_HEADDIM

    def per_head(a, h0, first):
        return jnp.where(first, a[:, h0:h0 + 1], a[:, h0 + 1:h0 + 2])

    ys = []
    for g in range(M_GROUPS):
        bm = mb[:, g * M_STATE:(g + 1) * M_STATE]
        cm = mc[:, g * M_STATE:(g + 1) * M_STATE]
        cb = lax.dot_general(cm, bm, NT, preferred_element_type=F32)
        for h0 in range(g * rep, (g + 1) * rep, 2):
            wts = []
            for hd in (h0, h0 + 1):
                seg = la[:, hd:hd + 1] - la_t[hd:hd + 1, :]
                dec = jnp.where(causal, jnp.exp(jnp.where(causal, seg, 0.0)), 0.0)
                wts.append((cb * dec * dt_t[hd:hd + 1, :]).astype(BF16))
            xp = mx[:, h0 * M_HEADDIM:(h0 + 2) * M_HEADDIM]
            x_diag = jnp.concatenate([jnp.where(first_lanes, xp, 0.0),
                                      jnp.where(first_lanes, 0.0, xp)], axis=0).astype(BF16)
            hst = h_ref[h0:h0 + 2].reshape(LANES, M_STATE)
            y2 = jnp.dot(jnp.concatenate(wts, axis=1), x_diag, preferred_element_type=F32)
            y2 = y2 + (lax.dot_general(cm, hst.astype(BF16), NT, preferred_element_type=F32)
                       * per_head(ela, h0, first_lanes))
            xw = (xp * per_head(w_end, h0, first_lanes)).astype(BF16)
            new = (per_head(e_last, h0, first_rows) * hst
                   + lax.dot_general(xw, bm, TN, preferred_element_type=F32))
            h_ref[h0:h0 + 2] = new.reshape(2, M_HEADDIM, M_STATE)
            ys.append(y2)
    y = jnp.concatenate(ys, axis=1) + dfull_ref[...] * mx
    mz = mz_ref[...]
    y = y * (mz * _sigmoid(mz))
    yssm_ref[...] = _rms(y, nw_ref[...]).astype(BF16)

    sbuf[8:8 + lc, :] = gc_ref[...] * gh_ref[...]
    ysc_ref[...] = (gb_ref[...] * _conv_taps(sbuf, scw_ref, lc, SC_CONV)).astype(BF16)
    _roll_tail(sbuf, c, nv, lc, SC_CONV)

    @pl.when(c == n_chunks - 1)
    def _():
        mc_ref[...] = cbuf[8 - (M_CONV - 1):8, :]
        sc_ref[...] = sbuf[8 - (SC_CONV - 1):8, :]


def _ssd(proj, dtp, cw, cb, dtb, alog, dfull, nw, scw, mc0, h0, sc0, nb, nc, lc, nv):
    m = proj.shape[0]
    n_heads, m_xbc, sc_w = h0.shape[1], cw.shape[1], scw.shape[1]
    m_inner = n_heads * M_HEADDIM

    def col(off, w):
        return pl.BlockSpec((lc, w), lambda b, c: (b * nc + c, off // w))

    def full(a):
        return pl.BlockSpec(a.shape, lambda b, c: (0,) * a.ndim)

    def per_seq(a):
        return pl.BlockSpec((None,) + a.shape[1:], lambda b, c: (b,) + (0,) * (a.ndim - 1))

    outs = [jax.ShapeDtypeStruct((m, m_inner), BF16), jax.ShapeDtypeStruct((m, sc_w), BF16),
            jax.ShapeDtypeStruct(mc0.shape, F32), jax.ShapeDtypeStruct(h0.shape, F32),
            jax.ShapeDtypeStruct(sc0.shape, F32)]
    return pl.pallas_call(
        functools.partial(_ssd_kernel, nv, lc, nc),
        grid=(nb, nc),
        in_specs=[col(COL_XBC, m_xbc), col(COL_MZ, m_inner),
                  pl.BlockSpec((lc, LANES), lambda b, c: (b * nc + c, 0)),
                  col(COL_GB, sc_w), col(COL_GC, sc_w), col(COL_GH, sc_w),
                  full(cw), full(cb), full(dtb), full(alog), full(dfull), full(nw), full(scw),
                  per_seq(mc0), per_seq(h0), per_seq(sc0)],
        out_specs=[pl.BlockSpec((lc, m_inner), lambda b, c: (b * nc + c, 0)),
                   pl.BlockSpec((lc, sc_w), lambda b, c: (b * nc + c, 0)),
                   per_seq(mc0), per_seq(h0), per_seq(sc0)],
        out_shape=outs,
        scratch_shapes=[pltpu.VMEM((8 + lc, m_xbc), F32), pltpu.VMEM((8 + lc, sc_w), F32)],
        compiler_params=_params(("parallel", "arbitrary")),
        name="ssd",
    )(proj, proj, dtp, proj, proj, proj, cw, cb, dtb, alog, dfull, nw, scw, mc0, h0, sc0)


def _merge_kernel(x_ref, gl_ref, yr_ref, yb_ref, ym_ref, yc_ref, wb_ref, wo_ref, nw_ref, out_ref):
    d = x_ref.shape[1]
    offs = [0]
    for y_ref in (yr_ref, yb_ref, ym_ref, yc_ref):
        offs.append(offs[-1] + y_ref.shape[1])
    merged = None
    for i, y_ref in enumerate((yr_ref, yb_ref, ym_ref, yc_ref)):
        br = jnp.dot(y_ref[...], wb_ref[offs[i]:offs[i + 1], :], preferred_element_type=F32)
        term = _sigmoid(gl_ref[:, i * d:(i + 1) * d]) * br
        merged = term if merged is None else merged + term
    o = jnp.dot(merged.astype(BF16), wo_ref[...], preferred_element_type=F32)
    out_ref[...] = x_ref[...] + _rms(o, nw_ref[...])


def _merge(x, proj, yr, yb, ym, yc, wb, wo, nw):
    m, d = x.shape
    tm = _pick_tile(m, 256, SAMPLE_CHUNK)

    def rows(a):
        return pl.BlockSpec((tm, a.shape[1]), lambda i: (i, 0))

    def full(a):
        return pl.BlockSpec(a.shape, lambda i: (0, 0))

    return pl.pallas_call(
        _merge_kernel,
        grid=(m // tm,),
        in_specs=[rows(x), pl.BlockSpec((tm, N_BRANCH * d), lambda i: (i, COL_GL // (N_BRANCH * d))),
                  rows(yr), rows(yb), rows(ym), rows(yc), full(wb), full(wo), full(nw)],
        out_specs=rows(x),
        out_shape=jax.ShapeDtypeStruct((m, d), F32),
        compiler_params=_params(("parallel",)),
        name="merge",
    )(x, proj, yr, yb, ym, yc, wb, wo, nw)


def _mlp_kernel(x_ref, nw_pre_ref, nw_post_ref, wu_ref, wd_ref, out_ref, h_scr, acc):
    j = pl.program_id(1)

    @pl.when(j == 0)
    def _():
        h_scr[...] = _rms(x_ref[...], nw_pre_ref[...]).astype(BF16)
        acc[...] = jnp.zeros_like(acc)

    f = jnp.dot(h_scr[...], wu_ref[...], preferred_element_type=F32)
    f = jnp.square(jnp.maximum(f, 0.0)).astype(BF16)
    acc[...] += jnp.dot(f, wd_ref[...], preferred_element_type=F32)

    @pl.when(j == pl.num_programs(1) - 1)
    def _():
        out_ref[...] = x_ref[...] + _rms(acc[...], nw_post_ref[...])


def _mlp(x, nw_pre, nw_post, wu, wd):
    m, d = x.shape
    d_ff = wu.shape[1]
    tm = _pick_tile(m, 1280, SAMPLE_CHUNK)
    tf = 512
    return pl.pallas_call(
        _mlp_kernel,
        grid=(m // tm, d_ff // tf),
        in_specs=[pl.BlockSpec((tm, d), lambda i, j: (i, 0)),
                  pl.BlockSpec((1, d), lambda i, j: (0, 0)),
                  pl.BlockSpec((1, d), lambda i, j: (0, 0)),
                  pl.BlockSpec((d, tf), lambda i, j: (0, j)),
                  pl.BlockSpec((tf, d), lambda i, j: (j, 0))],
        out_specs=pl.BlockSpec((tm, d), lambda i, j: (i, 0)),
        out_shape=jax.ShapeDtypeStruct((m, d), F32),
        scratch_shapes=[pltpu.VMEM((tm, d), BF16), pltpu.VMEM((tm, d), F32)],
        compiler_params=_params(("parallel", "arbitrary")),
        name="mlp",
    )(x, nw_pre, nw_post, wu, wd)


def _reorder_w_in(w_in_l, d_model):
    sizes = (512, 512, 512, 512, 512, 512, 512, d_model, d_model + 2 * M_GROUPS * M_STATE,
             d_model // M_HEADDIM, d_model // 2, d_model // 2, d_model // 2, N_BRANCH * d_model)
    parts, off = [], 0
    for s in sizes:
        parts.append(w_in_l[:, off:off + s])
        off += s
    rq, rk, rv, rg, sq, sk, sv, mz, xbc, dt, gb, gc, gh, gl = parts
    w_main = jnp.concatenate([gl, xbc, mz, rq, rk, rv, rg, sq, sk, sv, gb, gc, gh], axis=1).astype(BF16)
    w_dt = jnp.pad(dt, ((0, 0), (0, LANES - dt.shape[1]))).astype(BF16)
    return w_main, w_dt


def _rope_tables(nc, lc, nv, pos0):
    r = jnp.arange(nc * lc)
    c, i = r // lc, r % lc
    pos = jnp.where(c == 0, i, nv + (c - 1) * lc + i) + pos0
    half = RET_DK // 2
    inv = ROPE_BASE ** (-jnp.arange(half, dtype=F32) / half)
    ang = pos.astype(F32)[:, None] * inv[None, :]
    cos, sin = jnp.cos(ang), jnp.sin(ang)
    return jnp.concatenate([cos, cos], axis=1), jnp.concatenate([-sin, sin], axis=1)


def _pad_lanes(v):
    return jnp.pad(v.astype(F32), (0, LANES - v.shape[0]))[None, :]


def _layer(x, l, nb, nc, lc, nv, pos0, sb_fn, states, wts):
    (w_in, w_branch, w_o, w_up, w_down, norm_w, ret_gn_w, m_conv_w, m_conv_b, m_dt_bias, m_a_log,
     m_d, m_norm_w, sc_w, sb_bias) = wts
    s_ret, s_ssm, s_mconv, s_sconv = states
    d = x.shape[1]
    w_main, w_dt = _reorder_w_in(w_in[l], d)
    proj, dtp = _inproj(x, norm_w[l, 0][None, :], w_main, w_dt)
    cos2, sin2 = _rope_tables(nc, lc, nv, pos0)
    y_ret, o_ret = _retention(proj, cos2, sin2, ret_gn_w[l][None, :], s_ret, nb, nc, lc, nv)
    y_sb = sb_fn(proj, sb_bias[l].astype(F32))
    dfull = jnp.repeat(m_d[l].astype(F32), M_HEADDIM)[None, :]
    y_ssm, y_sc, o_mconv, o_ssm, o_sconv = _ssd(
        proj, dtp, m_conv_w[l], m_conv_b[l][None, :], _pad_lanes(m_dt_bias[l]), _pad_lanes(m_a_log[l]),
        dfull, m_norm_w[l][None, :], sc_w[l], s_mconv, s_ssm, s_sconv, nb, nc, lc, nv)
    x = _merge(x, proj, y_ret, y_sb, y_ssm, y_sc, w_branch[l].astype(BF16), w_o[l].astype(BF16),
               norm_w[l, 1][None, :])
    x = _mlp(x, norm_w[l, 2][None, :], norm_w[l, 3][None, :], w_up[l].astype(BF16), w_down[l].astype(BF16))
    return x, proj, (o_ret, o_ssm, o_mconv, o_sconv)


def _valid_rows(a3, nv, lc):
    if a3.shape[1] == lc:
        return a3[:, :nv]
    return a3[:, lc - nv:].at[:, :nv].set(a3[:, :nv])


def kernel(x_prompt, x_sample, cache_sb_k, cache_sb_v, state_ret, state_ssm, state_mconv, state_sconv,
           page_table, meta, w_in, w_branch, w_o, w_up, w_down, norm_w, ret_gn_w, m_conv_w, m_conv_b,
           m_dt_bias, m_a_log, m_d, m_norm_w, sc_w, sb_bias):
    nb_p, seq, d = x_prompt.shape
    nb_s, dec_len, _ = x_sample.shape
    depth = w_in.shape[0]
    assert seq % CHUNK == 0 and dec_len == 1 and meta.shape[0] == N_META
    past = page_table.shape[1] * cache_sb_k.shape[2]
    wts = (w_in, w_branch, w_o, w_up, w_down, norm_w, ret_gn_w, m_conv_w, m_conv_b, m_dt_bias, m_a_log,
           m_d, m_norm_w, sc_w, sb_bias)

    nc_p = 1 + seq // CHUNK
    xp = jnp.concatenate([jnp.broadcast_to(meta.astype(F32)[None], (nb_p, N_META, d)),
                          jnp.zeros((nb_p, CHUNK - N_META, d), F32), x_prompt], axis=1)
    xp = xp.reshape(nb_p * nc_p * CHUNK, d)
    lc_s = SAMPLE_CHUNK
    xs = jnp.pad(x_sample, ((0, 0), (0, lc_s - dec_len), (0, 0))).reshape(nb_s * lc_s, d)

    n_phys, page = cache_sb_k.shape[1], cache_sb_k.shape[2]
    ck = cache_sb_k.reshape(depth, n_phys, page * SB_HEADS, SB_DH)
    cv = cache_sb_v.reshape(depth, n_phys, page * SB_HEADS, SB_DH)

    zeros_p = (jnp.zeros((nb_p,) + state_ret.shape[2:], F32), jnp.zeros((nb_p,) + state_ssm.shape[2:], F32),
               jnp.zeros((nb_p,) + state_mconv.shape[2:], F32), jnp.zeros((nb_p,) + state_sconv.shape[2:], F32))
    p_out = [[] for _ in range(6)]
    s_out = [[] for _ in range(6)]
    for l in range(depth):
        xp, proj_p, st_p = _layer(
            xp, l, nb_p, nc_p, CHUNK, N_META, 0,
            lambda pr, bias: _sb_prompt(pr, bias, nb_p, nc_p, N_META), zeros_p, wts)
        xs, proj_s, st_s = _layer(
            xs, l, nb_s, 1, lc_s, dec_len, past,
            lambda pr, bias, l=l: _sb_sample(pr, bias, ck, cv, page_table, l, nb_s, lc_s, dec_len),
            (state_ret[l], state_ssm[l], state_mconv[l], state_sconv[l]), wts)
        for outs, proj, st, nb, nc, lc, nv in ((p_out, proj_p, st_p, nb_p, nc_p, CHUNK, N_META),
                                                (s_out, proj_s, st_s, nb_s, 1, lc_s, dec_len)):
            p3 = proj.reshape(nb, nc * lc, PROJ_W)
            width = SB_HEADS * SB_DH
            for i, off in enumerate((COL_SK, COL_SV)):
                kv = _valid_rows(p3[:, :, off:off + width], nv, lc)
                outs[i].append(kv.reshape(nb, kv.shape[1], SB_HEADS, SB_DH))
            for i in range(4):
                outs[2 + i].append(st[i])
    p_sb_k, p_sb_v, p_ret, p_ssm, p_mconv, p_sconv = [jnp.stack(a) for a in p_out]
    s_sb_k, s_sb_v, s_ret, s_ssm, s_mconv, s_sconv = [jnp.stack(a) for a in s_out]
    y_prompt = xp.reshape(nb_p, nc_p * CHUNK, d)[:, CHUNK:]
    y_sample = xs.reshape(nb_s, lc_s, d)[:, :dec_len]
    return (y_prompt, y_sample, p_sb_k, p_sb_v, p_ret, p_ssm, p_mconv, p_sconv,
            s_sb_k, s_sb_v, s_ret, s_ssm, s_mconv, s_sconv)
```

```python
import functools
import math

import jax
import jax.numpy as jnp
from jax import lax
from jax.experimental import pallas as pl
from jax.experimental.pallas import tpu as pltpu

F32 = jnp.float32
BF16 = jnp.bfloat16
HIGHEST = lax.Precision.HIGHEST

N_META = 16
CHUNK = 128
EPS = 1e-6
RET_HEADS = 4
RET_DK = 128
ROPE_BASE = 10000.0
SB_HEADS = 4
SB_DH = 128
M_HEADDIM = 64
M_GROUPS = 4
M_STATE = 128
M_CONV = 4
SC_CONV = 3
N_BRANCH = 4
SAMPLE_CHUNK = 16
LANES = 128
VMEM_LIMIT = 56 * 1024 * 1024

COL_GL, COL_XBC, COL_MZ = 0, 4096, 6144
COL_RQ, COL_RK, COL_RV, COL_RG = 7168, 7680, 8192, 8704
COL_SQ, COL_SK, COL_SV = 9216, 9728, 10240
COL_GB, COL_GC, COL_GH = 10752, 11264, 11776
PROJ_W = 12288

NT = (((1,), (1,)), ((), ()))
TN = (((0,), (0,)), ((), ()))


def _pick_tile(m, cap, mult):
    best = None
    for t in range(mult, min(m, cap) + 1, mult):
        if m % t == 0:
            best = t
    assert best is not None, (m, cap, mult)
    return best


def _sigmoid(x):
    return 1.0 / (1.0 + jnp.exp(-x))


def _softplus(x):
    return jnp.maximum(x, 0.0) + jnp.log1p(jnp.exp(-jnp.abs(x)))


def _rms(x, w):
    return x * lax.rsqrt(jnp.mean(x * x, axis=-1, keepdims=True) + EPS) * w


def _params(sem):
    return pltpu.CompilerParams(dimension_semantics=sem, vmem_limit_bytes=VMEM_LIMIT)


def _inproj_kernel(x_ref, nw_ref, w_ref, wdt_ref, out_ref, dt_ref, h_scr):
    @pl.when(pl.program_id(1) == 0)
    def _():
        hb = _rms(x_ref[...], nw_ref[...]).astype(BF16)
        h_scr[...] = hb
        dt_ref[...] = jnp.dot(hb, wdt_ref[...], preferred_element_type=F32)

    out_ref[...] = jnp.dot(h_scr[...], w_ref[...], preferred_element_type=F32)


def _inproj(x, nw, w_main, w_dt):
    m, d = x.shape
    tm = _pick_tile(m, 1280, SAMPLE_CHUNK)
    tn = 2048
    return pl.pallas_call(
        _inproj_kernel,
        grid=(m // tm, PROJ_W // tn),
        in_specs=[
            pl.BlockSpec((tm, d), lambda i, j: (i, 0)),
            pl.BlockSpec((1, d), lambda i, j: (0, 0)),
            pl.BlockSpec((d, tn), lambda i, j: (0, j)),
            pl.BlockSpec((d, LANES), lambda i, j: (0, 0)),
        ],
        out_specs=[
            pl.BlockSpec((tm, tn), lambda i, j: (i, j)),
            pl.BlockSpec((tm, LANES), lambda i, j: (i, 0)),
        ],
        out_shape=[jax.ShapeDtypeStruct((m, PROJ_W), F32), jax.ShapeDtypeStruct((m, LANES), F32)],
        scratch_shapes=[pltpu.VMEM((tm, d), BF16)],
        compiler_params=_params(("parallel", "arbitrary")),
        name="inproj",
    )(x, nw, w_main, w_dt)


def _ret_kernel(nv, lc, gc, q_ref, k_ref, v_ref, g_ref, cos_ref, sin_ref, gnw_ref, s0_ref, y_ref, s_ref):
    c = pl.program_id(1)

    @pl.when(c == 0)
    def _():
        s_ref[...] = s0_ref[...]

    row = lax.broadcasted_iota(jnp.int32, (lc, LANES), 0)
    rowf = row.astype(F32)
    diff = (lax.broadcasted_iota(jnp.int32, (lc, lc), 0)
            - lax.broadcasted_iota(jnp.int32, (lc, lc), 1)).astype(F32)
    for u in range(gc):
        rows = slice(u * lc, (u + 1) * lc)
        if u == 0:
            valid = jnp.logical_or(c > 0, row < nv)
            n_valid = jnp.where(c == 0, nv, lc).astype(F32)
        else:
            valid = None
            n_valid = float(lc)
        l_eff = jnp.full((lc, LANES), 1.0, F32) * n_valid
        l_eff_sq = jnp.full((LANES, LANES), 1.0, F32) * n_valid
        cos = cos_ref[rows, :]
        sin = sin_ref[rows, :]
        for h in range(RET_HEADS):
            lg = math.log1p(-(2.0 ** (-5 - h)))
            sl = slice(h * LANES, (h + 1) * LANES)
            q, k, v = q_ref[rows, sl], k_ref[rows, sl], v_ref[rows, sl]
            if valid is not None:
                q, k, v = jnp.where(valid, q, 0.0), jnp.where(valid, k, 0.0), jnp.where(valid, v, 0.0)
            v = v.astype(BF16)
            qr = ((q * cos + pltpu.roll(q, 64, 1) * sin) * (RET_DK ** -0.5)).astype(BF16)
            kr = k * cos + pltpu.roll(k, 64, 1) * sin
            decay = jnp.where(diff >= 0, jnp.exp(lg * jnp.maximum(diff, 0.0)), 0.0)
            scores = lax.dot_general(qr, kr.astype(BF16), NT, preferred_element_type=F32) * decay
            s = s_ref[h]
            o = jnp.dot(scores.astype(BF16), v, preferred_element_type=F32)
            o = o + jnp.dot(qr, s.astype(BF16), preferred_element_type=F32) * jnp.exp(lg * (rowf + 1.0))
            kd = (kr * jnp.exp(lg * (l_eff - 1.0 - rowf))).astype(BF16)
            s_ref[h] = jnp.exp(lg * l_eff_sq) * s + lax.dot_general(kd, v, TN, preferred_element_type=F32)
            mu = jnp.mean(o, axis=-1, keepdims=True)
            d = o - mu
            var = jnp.mean(d * d, axis=-1, keepdims=True)
            g = g_ref[rows, sl]
            y = d * lax.rsqrt(var + EPS) * gnw_ref[:, sl] * (g * _sigmoid(g))
            y_ref[rows, sl] = y.astype(BF16)


def _retention(proj, cos2, sin2, gnw, s0, nb, nc, lc, nv):
    m = proj.shape[0]
    w = RET_HEADS * LANES
    gc = max(c for c in range(1, 9) if nc % c == 0)
    steps = nc // gc

    def col(off):
        return pl.BlockSpec((gc * lc, w), lambda b, c: (b * steps + c, off // w))

    return pl.pallas_call(
        functools.partial(_ret_kernel, nv, lc, gc),
        grid=(nb, steps),
        in_specs=[
            col(COL_RQ), col(COL_RK), col(COL_RV), col(COL_RG),
            pl.BlockSpec((gc * lc, LANES), lambda b, c: (c, 0)),
            pl.BlockSpec((gc * lc, LANES), lambda b, c: (c, 0)),
            pl.BlockSpec((1, w), lambda b, c: (0, 0)),
            pl.BlockSpec((None, RET_HEADS, LANES, LANES), lambda b, c: (b, 0, 0, 0)),
        ],
        out_specs=[
            pl.BlockSpec((gc * lc, w), lambda b, c: (b * steps + c, 0)),
            pl.BlockSpec((None, RET_HEADS, LANES, LANES), lambda b, c: (b, 0, 0, 0)),
        ],
        out_shape=[jax.ShapeDtypeStruct((m, w), BF16),
                   jax.ShapeDtypeStruct((nb, RET_HEADS, LANES, LANES), F32)],
        compiler_params=_params(("parallel", "arbitrary")),
        name="retention",
    )(proj, proj, proj, proj, cos2, sin2, gnw, s0)


def _cumsum_matrix(n):
    j = lax.broadcasted_iota(jnp.int32, (n, n + LANES), 0)
    s = lax.broadcasted_iota(jnp.int32, (n, n + LANES), 1)
    return jnp.where(jnp.logical_or(j > s, s >= n), 1.0, 0.0).astype(BF16)


def _neg_abs(x):
    bits = lax.bitcast_convert_type(x, jnp.uint32) | jnp.uint32(0x80000000)
    return lax.bitcast_convert_type(bits, F32)


LOG2E = 1.0 / math.log(2.0)


def _stay_cost2(w):
    return jnp.maximum(w, 0.0) + jnp.log(1.0 + jnp.exp2(_neg_abs(w))) * LOG2E


def _sbp_kernel(nv, qc, bias_ref, q_ref, k_ref, v_ref, o_ref, qb_ref, acc_ref, carry_ref):
    h = pl.program_id(1)
    s = pl.program_id(2)
    bias = bias_ref[h] * LOG2E
    scale = SB_DH ** -0.5 * LOG2E
    rows = qc * CHUNK
    u1 = _cumsum_matrix(CHUNK)
    u2 = _cumsum_matrix(2 * CHUNK)[:, :2 * CHUNK]

    def one_block(r0, nr, j, mask):
        start = j * CHUNK if isinstance(j, int) else pl.multiple_of(j * CHUNK, CHUNK)
        kb = k_ref[pl.ds(start, CHUNK), :].astype(BF16)
        vb = v_ref[pl.ds(start, CHUNK), :].astype(BF16)
        w = lax.dot_general(qb_ref[r0:r0 + nr, :], kb, NT, preferred_element_type=F32) + bias
        cost = _stay_cost2(w)
        r = jnp.dot(jnp.where(mask, cost, 0.0).astype(BF16), u1, preferred_element_type=F32)
        c = carry_ref[r0:r0 + nr, :]
        a = jnp.where(mask, jnp.exp2(w - (cost + r[:, :CHUNK] + c)), 0.0)
        acc_ref[r0:r0 + nr, :] += jnp.dot(a.astype(BF16), vb, preferred_element_type=F32)
        carry_ref[r0:r0 + nr, :] = c + r[:, CHUNK:]

    def two_blocks(j_hi):
        start = pl.multiple_of((j_hi - 1) * CHUNK, CHUNK)
        kb = k_ref[pl.ds(start, 2 * CHUNK), :].astype(BF16)
        vb = v_ref[pl.ds(start, 2 * CHUNK), :].astype(BF16)
        w = lax.dot_general(qb_ref[...], kb, NT, preferred_element_type=F32) + bias
        cost = _stay_cost2(w)
        r = jnp.dot(cost.astype(BF16), u2, preferred_element_type=F32)
        c = carry_ref[...]
        a = jnp.exp2(w - (cost + r + jnp.concatenate([c, c], axis=1)))
        acc_ref[...] += jnp.dot(a.astype(BF16), vb, preferred_element_type=F32)
        carry_ref[...] = c + (r[:, 0:1] + cost[:, 0:1])

    ci = lax.broadcasted_iota(jnp.int32, (CHUNK, CHUNK), 1)
    ri = lax.broadcasted_iota(jnp.int32, (CHUNK, CHUNK), 0)

    @pl.when(s == 0)
    def _():
        qb_ref[0:CHUNK, :] = (q_ref[0:CHUNK, :] * scale).astype(BF16)
        acc_ref[0:CHUNK, :] = jnp.zeros((CHUNK, SB_DH), F32)
        carry_ref[0:CHUNK, :] = jnp.zeros((CHUNK, CHUNK), F32)
        one_block(0, CHUNK, 0, jnp.logical_and(ci < ri, ci < nv))
        o_ref[0:CHUNK, :] = acc_ref[0:CHUNK, :].astype(BF16)

    @pl.when(s > 0)
    def _():
        jb = 1 + (s - 1) * qc
        q0 = pl.multiple_of(jb * CHUNK, CHUNK)
        qb_ref[...] = (q_ref[pl.ds(q0, rows), :] * scale).astype(BF16)
        acc_ref[...] = jnp.zeros((rows, SB_DH), F32)
        carry_ref[...] = jnp.zeros((rows, CHUNK), F32)
        for m in range(qc - 1, -1, -1):
            nr = rows - m * CHUNK
            cm = lax.broadcasted_iota(jnp.int32, (nr, CHUNK), 1)
            rm = lax.broadcasted_iota(jnp.int32, (nr, CHUNK), 0)
            one_block(m * CHUNK, nr, jb + m, cm < rm)

        pairs = qc // 2

        def body(t, carry):
            for p in range(pairs):
                two_blocks(jb - 1 - 2 * (pairs * t + p))
            return carry

        lax.fori_loop(0, ((s - 1) * qc) // (2 * pairs), body, 0)
        ca = lax.broadcasted_iota(jnp.int32, (rows, CHUNK), 1)
        one_block(0, rows, 0, ca < nv)
        o_ref[pl.ds(q0, rows), :] = acc_ref[...].astype(BF16)


def _sb_prompt(proj, bias, nb, nc, nv):
    m = proj.shape[0]
    t_pad = nc * CHUNK
    qc = next(c for c in (8, 4, 2) if (nc - 1) % c == 0)
    rows = qc * CHUNK

    def seq(off):
        return pl.BlockSpec((t_pad, SB_DH), lambda b, h, s: (b, off // SB_DH + h))

    return pl.pallas_call(
        functools.partial(_sbp_kernel, nv, qc),
        grid=(nb, SB_HEADS, 1 + (nc - 1) // qc),
        in_specs=[pl.BlockSpec(memory_space=pltpu.SMEM), seq(COL_SQ), seq(COL_SK), seq(COL_SV)],
        out_specs=pl.BlockSpec((t_pad, SB_DH), lambda b, h, s: (b, h)),
        out_shape=jax.ShapeDtypeStruct((m, SB_HEADS * SB_DH), BF16),
        scratch_shapes=[pltpu.VMEM((rows, SB_DH), BF16), pltpu.VMEM((rows, SB_DH), F32),
                        pltpu.VMEM((rows, CHUNK), F32)],
        compiler_params=_params(("parallel", "parallel", "arbitrary")),
        name="sb_prompt",
    )(bias, proj, proj, proj)


SBS_PAGES = 16
HEAD_BITS = SB_HEADS.bit_length() - 1
assert 1 << HEAD_BITS == SB_HEADS


def _head_of(x):
    return jnp.bitwise_and(x, SB_HEADS - 1)


def _sbs_kernel(lc, n_steps, past, dec_len, pt_ref, bias_ref, q_ref, kn_ref, vn_ref, u_ref, *refs):
    del pt_ref
    pp = SBS_PAGES
    k_refs, v_refs = refs[:pp], refs[pp:2 * pp]
    o_ref, acc_ref, carry_ref = refs[2 * pp:]
    s = pl.program_id(1)
    rows = SB_HEADS * pp
    page_w = k_refs[0].shape[0]
    scale = SB_DH ** -0.5
    rr = lax.broadcasted_iota(jnp.int32, (rows, page_w), 0)
    nn = lax.broadcasted_iota(jnp.int32, (rows, page_w), 1)
    r8 = lax.broadcasted_iota(jnp.int32, (8, SB_DH), 0)

    @pl.when(s == 0)
    def _():
        kpos = past + lax.broadcasted_iota(jnp.int32, (1, 1), 1)
        visible = kpos < past + dec_len - 1
        acc0 = jnp.zeros((8, SB_DH), F32)
        carry0 = jnp.zeros((8, LANES), F32)
        for h in range(SB_HEADS):
            sl = slice(h * SB_DH, (h + 1) * SB_DH)
            z_new = (jnp.sum(q_ref[0:1, sl] * kn_ref[0:1, sl], axis=1, keepdims=True) * scale
                     + bias_ref[h])
            cost_new = _softplus(z_new)
            a_new = jnp.where(visible, jnp.exp(z_new - cost_new), 0.0)
            acc0 = jnp.where(r8 == h, a_new * vn_ref[0:1, sl], acc0)
            carry0 = jnp.where(_head_of(r8) == h, jnp.where(visible, cost_new, 0.0), carry0)
        acc_ref[...] = acc0
        carry_ref[...] = carry0

    r_q = lax.broadcasted_iota(jnp.int32, (rows, SB_DH), 0)
    qrep = jnp.zeros((rows, SB_DH), F32)
    for h in range(SB_HEADS):
        qrep = jnp.where(_head_of(r_q) == h, q_ref[0:1, h * SB_DH:(h + 1) * SB_DH], qrep)

    z = jnp.zeros((rows, page_w), F32)
    for i in range(pp):
        qsel = jnp.where(lax.shift_right_logical(r_q, HEAD_BITS) == i, qrep, 0.0).astype(BF16)
        z = z + lax.dot_general(qsel, k_refs[i][...].astype(BF16), NT, preferred_element_type=F32)

    own = _head_of(nn) == _head_of(rr)
    bias = jnp.zeros((rows, page_w), F32)
    for h in range(SB_HEADS):
        bias = jnp.where(_head_of(rr) == h, bias_ref[h], bias)
    z = z * scale + bias
    cost = jnp.where(own, _softplus(z), 0.0)
    r = jnp.dot(cost.astype(BF16), u_ref[...], preferred_element_type=F32)
    rest, tot = r[:, :page_w], r[:, page_w:]

    offs = []
    base = carry_ref[...]
    for g in range(pp // 2):
        t8 = tot[8 * g:8 * g + 8]
        swapped = pltpu.roll(t8, SB_HEADS, 0)
        offs.append(base + jnp.where(r8 >= SB_HEADS, swapped, 0.0))
        base = base + t8 + swapped
    carry_ref[...] = base
    off = jnp.concatenate(offs, axis=0)
    off = jnp.concatenate([off] * (page_w // LANES), axis=1)
    a = jnp.where(own, jnp.exp(z - (cost + rest + off)), 0.0).astype(BF16)

    acc = acc_ref[...]
    for g in range(pp // 2):
        a8 = a[8 * g:8 * g + 8]
        o0 = jnp.dot(a8, v_refs[2 * g][...].astype(BF16), preferred_element_type=F32)
        o1 = jnp.dot(a8, v_refs[2 * g + 1][...].astype(BF16), preferred_element_type=F32)
        acc = acc + jnp.where(r8 < SB_HEADS, o0, o1)
    acc_ref[...] = acc

    @pl.when(s == n_steps - 1)
    def _():
        tot8 = acc_ref[...]
        o4 = tot8[0:SB_HEADS] + tot8[SB_HEADS:2 * SB_HEADS]
        row0 = lax.broadcasted_iota(jnp.int32, (lc, SB_DH), 0) == 0
        for h in range(SB_HEADS):
            o_ref[:, h * SB_DH:(h + 1) * SB_DH] = jnp.where(row0, o4[h:h + 1], 0.0).astype(BF16)


def _sb_sample(proj, bias, cache_k, cache_v, page_table, layer, nb, lc, dec_len):
    m = proj.shape[0]
    n_pages = page_table.shape[1]
    pp = SBS_PAGES
    assert n_pages % pp == 0 and pp % 2 == 0 and SB_HEADS * 2 == 8
    n_steps = n_pages // pp
    page_rows = cache_k.shape[2]
    past = n_pages * (page_rows // SB_HEADS)
    width = SB_HEADS * SB_DH
    assert SB_DH == LANES
    u = _cumsum_matrix(page_rows)

    def page_spec(i):
        return pl.BlockSpec((None, None, page_rows, SB_DH),
                            lambda b, s, pt: (layer, pt[b, n_pages - 1 - (s * pp + i)], 0, 0))

    def col(off):
        return pl.BlockSpec((lc, width), lambda b, s, pt: (b, off // width))

    grid_spec = pltpu.PrefetchScalarGridSpec(
        num_scalar_prefetch=1,
        grid=(nb, n_steps),
        in_specs=[pl.BlockSpec(memory_space=pltpu.SMEM), col(COL_SQ), col(COL_SK), col(COL_SV),
                  pl.BlockSpec(u.shape, lambda b, s, pt: (0, 0))]
                 + [page_spec(i) for i in range(pp)] + [page_spec(i) for i in range(pp)],
        out_specs=pl.BlockSpec((lc, width), lambda b, s, pt: (b, 0)),
        scratch_shapes=[pltpu.VMEM((8, SB_DH), F32), pltpu.VMEM((8, LANES), F32)],
    )
    return pl.pallas_call(
        functools.partial(_sbs_kernel, lc, n_steps, past, dec_len),
        grid_spec=grid_spec,
        out_shape=jax.ShapeDtypeStruct((m, width), BF16),
        compiler_params=_params(("parallel", "arbitrary")),
        name="sb_sample",
    )(page_table, bias, proj, proj, proj, u, *([cache_k] * pp), *([cache_v] * pp))


def _conv_taps(buf, w_ref, lc, taps):
    full = buf[0:8 + lc, :]
    out = full * w_ref[0:1, :]
    for k in range(1, taps):
        out = _shift_down_one(out) + full * w_ref[k:k + 1, :]
    return out[8:8 + lc, :]


def _shift_down_one(x):
    n, c = x.shape
    first = lax.broadcasted_iota(jnp.int32, (8, c), 0) == 0
    rots = [pltpu.roll(x[8 * i:8 * i + 8, :], 1, 0) for i in range(n // 8)]
    return jnp.concatenate([jnp.where(first, rots[i - 1], rots[i]) for i in range(n // 8)], axis=0)


def _roll_tail(buf, c, nv, lc, taps):
    base = 8 - (taps - 1)

    @pl.when(c == 0)
    def _():
        buf[base:8, :] = buf[base + nv:8 + nv, :]

    @pl.when(c > 0)
    def _():
        buf[base:8, :] = buf[base + lc:8 + lc, :]


def _ssd_kernel(nv, lc, n_chunks,
                xbc_ref, mz_ref, dt_ref, gb_ref, gc_ref, gh_ref,
                cw_ref, cb_ref, dtb_ref, alog_ref, dfull_ref, nw_ref, scw_ref,
                mc0_ref, h0_ref, sc0_ref,
                yssm_ref, ysc_ref, mc_ref, h_ref, sc_ref,
                cbuf, sbuf):
    c = pl.program_id(1)
    n_heads = h_ref.shape[0]
    m_inner = n_heads * M_HEADDIM
    rep = n_heads // M_GROUPS

    @pl.when(c == 0)
    def _():
        cbuf[0:8, :] = jnp.zeros((8, cbuf.shape[1]), F32)
        sbuf[0:8, :] = jnp.zeros((8, sbuf.shape[1]), F32)
        cbuf[8 - (M_CONV - 1):8, :] = mc0_ref[...]
        sbuf[8 - (SC_CONV - 1):8, :] = sc0_ref[...]
        h_ref[...] = h0_ref[...]

    cbuf[8:8 + lc, :] = xbc_ref[...]
    conv = _conv_taps(cbuf, cw_ref, lc, M_CONV) + cb_ref[...]
    _roll_tail(cbuf, c, nv, lc, M_CONV)
    xc = conv * _sigmoid(conv)
    mx = xc[:, :m_inner]
    mb = xc[:, m_inner:m_inner + M_GROUPS * M_STATE].astype(BF16)
    mc = xc[:, m_inner + M_GROUPS * M_STATE:].astype(BF16)

    row = lax.broadcasted_iota(jnp.int32, (lc, LANES), 0)
    valid = jnp.logical_or(c > 0, row < nv)
    dt = jnp.where(valid, _softplus(dt_ref[...] + dtb_ref[...]), 0.0)
    dta = dt * (-jnp.exp(alog_ref[...]))
    ii = lax.broadcasted_iota(jnp.int32, (lc, lc), 0)
    jj = lax.broadcasted_iota(jnp.int32, (lc, lc), 1)
    causal = ii >= jj
    la = jnp.dot(jnp.where(causal, 1.0, 0.0), dta, precision=HIGHEST, preferred_element_type=F32)

    def transposed(a):
        if lc < LANES:
            a = jnp.concatenate([a, jnp.zeros((LANES - lc, LANES), F32)], axis=0)
        return a.T[:, :lc]

    la_t = transposed(la)
    dt_t = transposed(dt)
    la_last = la[lc - 1:lc, :]
    ela = jnp.exp(la)
    w_end = jnp.exp(la_last - la) * dt
    e_last = jnp.exp(la_last)

    assert 2 * M_HEADDIM == LANES and rep % 2 == 0
    first_lanes = lax.broadcasted_iota(jnp.int32, (lc, LANES), 1) < M_HEADDIM
    first_rows = lax.broadcasted_iota(jnp.int32, (LANES, M_STATE), 0) < M_HEADDIM

    def per_head(a, h0, first):
        return jnp.where(first, a[:, h0:h0 + 1], a[:, h0 + 1:h0 + 2])

    ys = []
    for g in range(M_GROUPS):
        bm = mb[:, g * M_STATE:(g + 1) * M_STATE]
        cm = mc[:, g * M_STATE:(g + 1) * M_STATE]
        cb = lax.dot_general(cm, bm, NT, preferred_element_type=F32)
        for h0 in range(g * rep, (g + 1) * rep, 2):
            wts = []
            for hd in (h0, h0 + 1):
                seg = la[:, hd:hd + 1] - la_t[hd:hd + 1, :]
                dec = jnp.where(causal, jnp.exp(jnp.where(causal, seg, 0.0)), 0.0)
                wts.append((cb * dec * dt_t[hd:hd + 1, :]).astype(BF16))
            xp = mx[:, h0 * M_HEADDIM:(h0 + 2) * M_HEADDIM]
            x_diag = jnp.concatenate([jnp.where(first_lanes, xp, 0.0),
                                      jnp.where(first_lanes, 0.0, xp)], axis=0).astype(BF16)
            hst = h_ref[h0:h0 + 2].reshape(LANES, M_STATE)
            y2 = jnp.dot(jnp.concatenate(wts, axis=1), x_diag, preferred_element_type=F32)
            y2 = y2 + (lax.dot_general(cm, hst.astype(BF16), NT, preferred_element_type=F32)
                       * per_head(ela, h0, first_lanes))
            xw = (xp * per_head(w_end, h0, first_lanes)).astype(BF16)
            new = (per_head(e_last, h0, first_rows) * hst
                   + lax.dot_general(xw, bm, TN, preferred_element_type=F32))
            h_ref[h0:h0 + 2] = new.reshape(2, M_HEADDIM, M_STATE)
            ys.append(y2)
    y = jnp.concatenate(ys, axis=1) + dfull_ref[...] * mx
    mz = mz_ref[...]
    y = y * (mz * _sigmoid(mz))
    yssm_ref[...] = _rms(y, nw_ref[...]).astype(BF16)

    sbuf[8:8 + lc, :] = gc_ref[...] * gh_ref[...]
    ysc_ref[...] = (gb_ref[...] * _conv_taps(sbuf, scw_ref, lc, SC_CONV)).astype(BF16)
    _roll_tail(sbuf, c, nv, lc, SC_CONV)

    @pl.when(c == n_chunks - 1)
    def _():
        mc_ref[...] = cbuf[8 - (M_CONV - 1):8, :]
        sc_ref[...] = sbuf[8 - (SC_CONV - 1):8, :]


def _ssd(proj, dtp, cw, cb, dtb, alog, dfull, nw, scw, mc0, h0, sc0, nb, nc, lc, nv):
    m = proj.shape[0]
    n_heads, m_xbc, sc_w = h0.shape[1], cw.shape[1], scw.shape[1]
    m_inner = n_heads * M_HEADDIM

    def col(off, w):
        return pl.BlockSpec((lc, w), lambda b, c: (b * nc + c, off // w))

    def full(a):
        return pl.BlockSpec(a.shape, lambda b, c: (0,) * a.ndim)

    def per_seq(a):
        return pl.BlockSpec((None,) + a.shape[1:], lambda b, c: (b,) + (0,) * (a.ndim - 1))

    outs = [jax.ShapeDtypeStruct((m, m_inner), BF16), jax.ShapeDtypeStruct((m, sc_w), BF16),
            jax.ShapeDtypeStruct(mc0.shape, F32), jax.ShapeDtypeStruct(h0.shape, F32),
            jax.ShapeDtypeStruct(sc0.shape, F32)]
    return pl.pallas_call(
        functools.partial(_ssd_kernel, nv, lc, nc),
        grid=(nb, nc),
        in_specs=[col(COL_XBC, m_xbc), col(COL_MZ, m_inner),
                  pl.BlockSpec((lc, LANES), lambda b, c: (b * nc + c, 0)),
                  col(COL_GB, sc_w), col(COL_GC, sc_w), col(COL_GH, sc_w),
                  full(cw), full(cb), full(dtb), full(alog), full(dfull), full(nw), full(scw),
                  per_seq(mc0), per_seq(h0), per_seq(sc0)],
        out_specs=[pl.BlockSpec((lc, m_inner), lambda b, c: (b * nc + c, 0)),
                   pl.BlockSpec((lc, sc_w), lambda b, c: (b * nc + c, 0)),
                   per_seq(mc0), per_seq(h0), per_seq(sc0)],
        out_shape=outs,
        scratch_shapes=[pltpu.VMEM((8 + lc, m_xbc), F32), pltpu.VMEM((8 + lc, sc_w), F32)],
        compiler_params=_params(("parallel", "arbitrary")),
        name="ssd",
    )(proj, proj, dtp, proj, proj, proj, cw, cb, dtb, alog, dfull, nw, scw, mc0, h0, sc0)


def _merge_kernel(x_ref, gl_ref, yr_ref, yb_ref, ym_ref, yc_ref, wb_ref, wo_ref, nw_ref, out_ref):
    d = x_ref.shape[1]
    offs = [0]
    for y_ref in (yr_ref, yb_ref, ym_ref, yc_ref):
        offs.append(offs[-1] + y_ref.shape[1])
    merged = None
    for i, y_ref in enumerate((yr_ref, yb_ref, ym_ref, yc_ref)):
        br = jnp.dot(y_ref[...], wb_ref[offs[i]:offs[i + 1], :], preferred_element_type=F32)
        term = _sigmoid(gl_ref[:, i * d:(i + 1) * d]) * br
        merged = term if merged is None else merged + term
    o = jnp.dot(merged.astype(BF16), wo_ref[...], preferred_element_type=F32)
    out_ref[...] = x_ref[...] + _rms(o, nw_ref[...])


def _merge(x, proj, yr, yb, ym, yc, wb, wo, nw):
    m, d = x.shape
    tm = _pick_tile(m, 512, SAMPLE_CHUNK)

    def rows(a):
        return pl.BlockSpec((tm, a.shape[1]), lambda i: (i, 0))

    def full(a):
        return pl.BlockSpec(a.shape, lambda i: (0, 0))

    return pl.pallas_call(
        _merge_kernel,
        grid=(m // tm,),
        in_specs=[rows(x), pl.BlockSpec((tm, N_BRANCH * d), lambda i: (i, COL_GL // (N_BRANCH * d))),
                  rows(yr), rows(yb), rows(ym), rows(yc), full(wb), full(wo), full(nw)],
        out_specs=rows(x),
        out_shape=jax.ShapeDtypeStruct((m, d), F32),
        compiler_params=_params(("parallel",)),
        name="merge",
    )(x, proj, yr, yb, ym, yc, wb, wo, nw)


def _mlp_kernel(x_ref, nw_pre_ref, nw_post_ref, wu_ref, wd_ref, out_ref, h_scr, acc):
    j = pl.program_id(1)

    @pl.when(j == 0)
    def _():
        h_scr[...] = _rms(x_ref[...], nw_pre_ref[...]).astype(BF16)
        acc[...] = jnp.zeros_like(acc)

    f = jnp.dot(h_scr[...], wu_ref[...], preferred_element_type=F32)
    f = jnp.square(jnp.maximum(f, 0.0)).astype(BF16)
    acc[...] += jnp.dot(f, wd_ref[...], preferred_element_type=F32)

    @pl.when(j == pl.num_programs(1) - 1)
    def _():
        out_ref[...] = x_ref[...] + _rms(acc[...], nw_post_ref[...])


def _mlp(x, nw_pre, nw_post, wu, wd):
    m, d = x.shape
    d_ff = wu.shape[1]
    tm = _pick_tile(m, 1280, SAMPLE_CHUNK)
    tf = 1024
    return pl.pallas_call(
        _mlp_kernel,
        grid=(m // tm, d_ff // tf),
        in_specs=[pl.BlockSpec((tm, d), lambda i, j: (i, 0)),
                  pl.BlockSpec((1, d), lambda i, j: (0, 0)),
                  pl.BlockSpec((1, d), lambda i, j: (0, 0)),
                  pl.BlockSpec((d, tf), lambda i, j: (0, j)),
                  pl.BlockSpec((tf, d), lambda i, j: (j, 0))],
        out_specs=pl.BlockSpec((tm, d), lambda i, j: (i, 0)),
        out_shape=jax.ShapeDtypeStruct((m, d), F32),
        scratch_shapes=[pltpu.VMEM((tm, d), BF16), pltpu.VMEM((tm, d), F32)],
        compiler_params=_params(("parallel", "arbitrary")),
        name="mlp",
    )(x, nw_pre, nw_post, wu, wd)


def _reorder_w_in(w_in_l, d_model):
    sizes = (512, 512, 512, 512, 512, 512, 512, d_model, d_model + 2 * M_GROUPS * M_STATE,
             d_model // M_HEADDIM, d_model // 2, d_model // 2, d_model // 2, N_BRANCH * d_model)
    parts, off = [], 0
    for s in sizes:
        parts.append(w_in_l[:, off:off + s])
        off += s
    rq, rk, rv, rg, sq, sk, sv, mz, xbc, dt, gb, gc, gh, gl = parts
    w_main = jnp.concatenate([gl, xbc, mz, rq, rk, rv, rg, sq, sk, sv, gb, gc, gh], axis=1).astype(BF16)
    w_dt = jnp.pad(dt, ((0, 0), (0, LANES - dt.shape[1]))).astype(BF16)
    return w_main, w_dt


def _rope_tables(nc, lc, nv, pos0):
    r = jnp.arange(nc * lc)
    c, i = r // lc, r % lc
    pos = jnp.where(c == 0, i, nv + (c - 1) * lc + i) + pos0
    half = RET_DK // 2
    inv = ROPE_BASE ** (-jnp.arange(half, dtype=F32) / half)
    ang = pos.astype(F32)[:, None] * inv[None, :]
    cos, sin = jnp.cos(ang), jnp.sin(ang)
    return jnp.concatenate([cos, cos], axis=1), jnp.concatenate([-sin, sin], axis=1)


def _pad_lanes(v):
    return jnp.pad(v.astype(F32), (0, LANES - v.shape[0]))[None, :]


def _layer(x, l, nb, nc, lc, nv, pos0, sb_fn, states, wts):
    (w_in, w_branch, w_o, w_up, w_down, norm_w, ret_gn_w, m_conv_w, m_conv_b, m_dt_bias, m_a_log,
     m_d, m_norm_w, sc_w, sb_bias) = wts
    s_ret, s_ssm, s_mconv, s_sconv = states
    d = x.shape[1]
    w_main, w_dt = _reorder_w_in(w_in[l], d)
    proj, dtp = _inproj(x, norm_w[l, 0][None, :], w_main, w_dt)
    cos2, sin2 = _rope_tables(nc, lc, nv, pos0)
    y_ret, o_ret = _retention(proj, cos2, sin2, ret_gn_w[l][None, :], s_ret, nb, nc, lc, nv)
    y_sb = sb_fn(proj, sb_bias[l].astype(F32))
    dfull = jnp.repeat(m_d[l].astype(F32), M_HEADDIM)[None, :]
    y_ssm, y_sc, o_mconv, o_ssm, o_sconv = _ssd(
        proj, dtp, m_conv_w[l], m_conv_b[l][None, :], _pad_lanes(m_dt_bias[l]), _pad_lanes(m_a_log[l]),
        dfull, m_norm_w[l][None, :], sc_w[l], s_mconv, s_ssm, s_sconv, nb, nc, lc, nv)
    x = _merge(x, proj, y_ret, y_sb, y_ssm, y_sc, w_branch[l].astype(BF16), w_o[l].astype(BF16),
               norm_w[l, 1][None, :])
    x = _mlp(x, norm_w[l, 2][None, :], norm_w[l, 3][None, :], w_up[l].astype(BF16), w_down[l].astype(BF16))
    return x, proj, (o_ret, o_ssm, o_mconv, o_sconv)


def _kv_rows_kernel(nv, g, *refs):
    depth = (len(refs) - 2) // 2
    k_out, v_out = refs[-2:]
    layer = pl.program_id(0)
    c = pl.program_id(3)
    span = g * CHUNK
    for l in range(depth):
        for src, dst in ((refs[2 * l], k_out), (refs[2 * l + 1], v_out)):
            @pl.when(jnp.logical_and(layer == l, c == 0))
            def _(src=src, dst=dst):
                dst[0:nv, :] = src[0:nv, :]
                if g > 1:
                    dst[nv:nv + span - CHUNK, :] = src[CHUNK:span, :]

            @pl.when(jnp.logical_and(layer == l, c > 0))
            def _(src=src, dst=dst):
                start = pl.multiple_of(nv + c * span - CHUNK, 8)
                dst[pl.ds(start, span), :] = src[...]


def _kv_rows(projs, nb, nc, nv):
    depth = len(projs)
    t_valid = nv + (nc - 1) * CHUNK
    width = SB_HEADS * SB_DH
    half = width // 2
    assert nv % 8 == 0
    g = max(c for c in range(1, 17) if nc % c == 0)
    steps = nc // g

    def src(l, off):
        return pl.BlockSpec((g * CHUNK, half),
                            lambda d, b, h, c: (jnp.where(d == l, b * steps + c, 0), off // half + h))

    in_specs, args = [], []
    for l in range(depth):
        in_specs += [src(l, COL_SK), src(l, COL_SV)]
        args += [projs[l], projs[l]]
    out_spec = pl.BlockSpec((None, None, t_valid, half), lambda d, b, h, c: (d, b, 0, h))
    shape = jax.ShapeDtypeStruct((depth, nb, t_valid, width), F32)
    return pl.pallas_call(
        functools.partial(_kv_rows_kernel, nv, g),
        grid=(depth, nb, 2, steps),
        in_specs=in_specs,
        out_specs=[out_spec, out_spec],
        out_shape=[shape, shape],
        compiler_params=_params(("parallel", "parallel", "parallel", "arbitrary")),
        name="kv_rows",
    )(*args)


def kernel(x_prompt, x_sample, cache_sb_k, cache_sb_v, state_ret, state_ssm, state_mconv, state_sconv,
           page_table, meta, w_in, w_branch, w_o, w_up, w_down, norm_w, ret_gn_w, m_conv_w, m_conv_b,
           m_dt_bias, m_a_log, m_d, m_norm_w, sc_w, sb_bias):
    nb_p, seq, d = x_prompt.shape
    nb_s, dec_len, _ = x_sample.shape
    depth = w_in.shape[0]
    assert seq % CHUNK == 0 and dec_len == 1 and meta.shape[0] == N_META
    past = page_table.shape[1] * cache_sb_k.shape[2]
    wts = (w_in, w_branch, w_o, w_up, w_down, norm_w, ret_gn_w, m_conv_w, m_conv_b, m_dt_bias, m_a_log,
           m_d, m_norm_w, sc_w, sb_bias)

    nc_p = 1 + seq // CHUNK
    xp = jnp.concatenate([jnp.broadcast_to(meta.astype(F32)[None], (nb_p, N_META, d)),
                          jnp.zeros((nb_p, CHUNK - N_META, d), F32), x_prompt], axis=1)
    xp = xp.reshape(nb_p * nc_p * CHUNK, d)
    lc_s = SAMPLE_CHUNK
    xs = jnp.pad(x_sample, ((0, 0), (0, lc_s - dec_len), (0, 0))).reshape(nb_s * lc_s, d)

    n_phys, page = cache_sb_k.shape[1], cache_sb_k.shape[2]
    ck = cache_sb_k.reshape(depth, n_phys, page * SB_HEADS, SB_DH)
    cv = cache_sb_v.reshape(depth, n_phys, page * SB_HEADS, SB_DH)

    zeros_p = (jnp.zeros((nb_p,) + state_ret.shape[2:], F32), jnp.zeros((nb_p,) + state_ssm.shape[2:], F32),
               jnp.zeros((nb_p,) + state_mconv.shape[2:], F32), jnp.zeros((nb_p,) + state_sconv.shape[2:], F32))
    p_out = [[] for _ in range(4)]
    s_out = [[] for _ in range(6)]
    projs_p = []
    width = SB_HEADS * SB_DH
    for l in range(depth):
        xp, proj_p, st_p = _layer(
            xp, l, nb_p, nc_p, CHUNK, N_META, 0,
            lambda pr, bias: _sb_prompt(pr, bias, nb_p, nc_p, N_META), zeros_p, wts)
        xs, proj_s, st_s = _layer(
            xs, l, nb_s, 1, lc_s, dec_len, past,
            lambda pr, bias, l=l: _sb_sample(pr, bias, ck, cv, page_table, l, nb_s, lc_s, dec_len),
            (state_ret[l], state_ssm[l], state_mconv[l], state_sconv[l]), wts)
        projs_p.append(proj_p)
        p3 = proj_s.reshape(nb_s, lc_s, PROJ_W)[:, :dec_len]
        for i, off in enumerate((COL_SK, COL_SV)):
            s_out[i].append(p3[:, :, off:off + width].reshape(nb_s, dec_len, SB_HEADS, SB_DH))
        for i in range(4):
            p_out[i].append(st_p[i])
            s_out[2 + i].append(st_s[i])
    p_sb_k, p_sb_v = [a.reshape(a.shape[:3] + (SB_HEADS, SB_DH))
                      for a in _kv_rows(projs_p, nb_p, nc_p, N_META)]
    p_ret, p_ssm, p_mconv, p_sconv = [jnp.stack(a) for a in p_out]
    s_sb_k, s_sb_v, s_ret, s_ssm, s_mconv, s_sconv = [jnp.stack(a) for a in s_out]
    y_prompt = xp.reshape(nb_p, nc_p * CHUNK, d)[:, CHUNK:]
    y_sample = xs.reshape(nb_s, lc_s, d)[:, :dec_len]
    return (y_prompt, y_sample, p_sb_k, p_sb_v, p_ret, p_ssm, p_mconv, p_sconv,
            s_sb_k, s_sb_v, s_ret, s_ssm, s_mconv, s_sconv)
```

```python
import functools
import math

import jax
import jax.numpy as jnp
from jax import lax
from jax.experimental import pallas as pl
from jax.experimental.pallas import tpu as pltpu

F32 = jnp.float32
BF16 = jnp.bfloat16
HIGHEST = lax.Precision.HIGHEST

N_META = 16
CHUNK = 128
EPS = 1e-6
RET_HEADS = 4
RET_DK = 128
ROPE_BASE = 10000.0
SB_HEADS = 4
SB_DH = 128
M_HEADDIM = 64
M_GROUPS = 4
M_STATE = 128
M_CONV = 4
SC_CONV = 3
N_BRANCH = 4
SAMPLE_CHUNK = 16
LANES = 128
VMEM_LIMIT = 56 * 1024 * 1024

COL_GL, COL_XBC, COL_MZ = 0, 4096, 6144
COL_RQ, COL_RK, COL_RV, COL_RG = 7168, 7680, 8192, 8704
COL_SQ, COL_SK, COL_SV = 9216, 9728, 10240
COL_GB, COL_GC, COL_GH = 10752, 11264, 11776
PROJ_W = 12288

NT = (((1,), (1,)), ((), ()))
TN = (((0,), (0,)), ((), ()))


def _pick_tile(m, cap, mult):
    best = None
    for t in range(mult, min(m, cap) + 1, mult):
        if m % t == 0:
            best = t
    assert best is not None, (m, cap, mult)
    return best


def _sigmoid(x):
    return 1.0 / (1.0 + jnp.exp(-x))


def _softplus(x):
    return jnp.maximum(x, 0.0) + jnp.log1p(jnp.exp(-jnp.abs(x)))


def _rms(x, w):
    return x * lax.rsqrt(jnp.mean(x * x, axis=-1, keepdims=True) + EPS) * w


def _params(sem):
    return pltpu.CompilerParams(dimension_semantics=sem, vmem_limit_bytes=VMEM_LIMIT)


def _inproj_kernel(x_ref, nw_ref, w_ref, wdt_ref, out_ref, dt_ref, h_scr):
    @pl.when(pl.program_id(1) == 0)
    def _():
        hb = _rms(x_ref[...], nw_ref[...]).astype(BF16)
        h_scr[...] = hb
        dt_ref[...] = jnp.dot(hb, wdt_ref[...], preferred_element_type=F32)

    out_ref[...] = jnp.dot(h_scr[...], w_ref[...], preferred_element_type=F32)


def _inproj(x, nw, w_main, w_dt):
    m, d = x.shape
    tm = _pick_tile(m, 1280, SAMPLE_CHUNK)
    tn = 2048
    return pl.pallas_call(
        _inproj_kernel,
        grid=(m // tm, PROJ_W // tn),
        in_specs=[
            pl.BlockSpec((tm, d), lambda i, j: (i, 0)),
            pl.BlockSpec((1, d), lambda i, j: (0, 0)),
            pl.BlockSpec((d, tn), lambda i, j: (0, j)),
            pl.BlockSpec((d, LANES), lambda i, j: (0, 0)),
        ],
        out_specs=[
            pl.BlockSpec((tm, tn), lambda i, j: (i, j)),
            pl.BlockSpec((tm, LANES), lambda i, j: (i, 0)),
        ],
        out_shape=[jax.ShapeDtypeStruct((m, PROJ_W), F32), jax.ShapeDtypeStruct((m, LANES), F32)],
        scratch_shapes=[pltpu.VMEM((tm, d), BF16)],
        compiler_params=_params(("parallel", "arbitrary")),
        name="inproj",
    )(x, nw, w_main, w_dt)


def _ret_kernel(nv, lc, gc, q_ref, k_ref, v_ref, g_ref, cos_ref, sin_ref, gnw_ref, s0_ref, y_ref, s_ref):
    c = pl.program_id(1)

    @pl.when(c == 0)
    def _():
        s_ref[...] = s0_ref[...]

    row = lax.broadcasted_iota(jnp.int32, (lc, LANES), 0)
    rowf = row.astype(F32)
    diff = (lax.broadcasted_iota(jnp.int32, (lc, lc), 0)
            - lax.broadcasted_iota(jnp.int32, (lc, lc), 1)).astype(F32)
    for u in range(gc):
        rows = slice(u * lc, (u + 1) * lc)
        if u == 0:
            valid = jnp.logical_or(c > 0, row < nv)
            n_valid = jnp.where(c == 0, nv, lc).astype(F32)
        else:
            valid = None
            n_valid = float(lc)
        l_eff = jnp.full((lc, LANES), 1.0, F32) * n_valid
        l_eff_sq = jnp.full((LANES, LANES), 1.0, F32) * n_valid
        cos = cos_ref[rows, :]
        sin = sin_ref[rows, :]
        for h in range(RET_HEADS):
            lg = math.log1p(-(2.0 ** (-5 - h)))
            sl = slice(h * LANES, (h + 1) * LANES)
            q, k, v = q_ref[rows, sl], k_ref[rows, sl], v_ref[rows, sl]
            if valid is not None:
                q, k, v = jnp.where(valid, q, 0.0), jnp.where(valid, k, 0.0), jnp.where(valid, v, 0.0)
            v = v.astype(BF16)
            qr = ((q * cos + pltpu.roll(q, 64, 1) * sin) * (RET_DK ** -0.5)).astype(BF16)
            kr = k * cos + pltpu.roll(k, 64, 1) * sin
            decay = jnp.where(diff >= 0, jnp.exp(lg * jnp.maximum(diff, 0.0)), 0.0)
            scores = lax.dot_general(qr, kr.astype(BF16), NT, preferred_element_type=F32) * decay
            s = s_ref[h]
            o = jnp.dot(scores.astype(BF16), v, preferred_element_type=F32)
            o = o + jnp.dot(qr, s.astype(BF16), preferred_element_type=F32) * jnp.exp(lg * (rowf + 1.0))
            kd = (kr * jnp.exp(lg * (l_eff - 1.0 - rowf))).astype(BF16)
            s_ref[h] = jnp.exp(lg * l_eff_sq) * s + lax.dot_general(kd, v, TN, preferred_element_type=F32)
            mu = jnp.mean(o, axis=-1, keepdims=True)
            d = o - mu
            var = jnp.mean(d * d, axis=-1, keepdims=True)
            g = g_ref[rows, sl]
            y = d * lax.rsqrt(var + EPS) * gnw_ref[:, sl] * (g * _sigmoid(g))
            y_ref[rows, sl] = y.astype(BF16)


def _retention(proj, cos2, sin2, gnw, s0, nb, nc, lc, nv):
    m = proj.shape[0]
    w = RET_HEADS * LANES
    gc = max(c for c in range(1, 9) if nc % c == 0)
    steps = nc // gc

    def col(off):
        return pl.BlockSpec((gc * lc, w), lambda b, c: (b * steps + c, off // w))

    return pl.pallas_call(
        functools.partial(_ret_kernel, nv, lc, gc),
        grid=(nb, steps),
        in_specs=[
            col(COL_RQ), col(COL_RK), col(COL_RV), col(COL_RG),
            pl.BlockSpec((gc * lc, LANES), lambda b, c: (c, 0)),
            pl.BlockSpec((gc * lc, LANES), lambda b, c: (c, 0)),
            pl.BlockSpec((1, w), lambda b, c: (0, 0)),
            pl.BlockSpec((None, RET_HEADS, LANES, LANES), lambda b, c: (b, 0, 0, 0)),
        ],
        out_specs=[
            pl.BlockSpec((gc * lc, w), lambda b, c: (b * steps + c, 0)),
            pl.BlockSpec((None, RET_HEADS, LANES, LANES), lambda b, c: (b, 0, 0, 0)),
        ],
        out_shape=[jax.ShapeDtypeStruct((m, w), BF16),
                   jax.ShapeDtypeStruct((nb, RET_HEADS, LANES, LANES), F32)],
        compiler_params=_params(("parallel", "arbitrary")),
        name="retention",
    )(proj, proj, proj, proj, cos2, sin2, gnw, s0)


def _cumsum_matrix(n):
    j = lax.broadcasted_iota(jnp.int32, (n, n + LANES), 0)
    s = lax.broadcasted_iota(jnp.int32, (n, n + LANES), 1)
    return jnp.where(jnp.logical_or(j > s, s >= n), 1.0, 0.0).astype(BF16)


def _neg_abs(x):
    bits = lax.bitcast_convert_type(x, jnp.uint32) | jnp.uint32(0x80000000)
    return lax.bitcast_convert_type(bits, F32)


LOG2E = 1.0 / math.log(2.0)


def _stay_cost2(w):
    return jnp.maximum(w, 0.0) + jnp.log(1.0 + jnp.exp2(_neg_abs(w))) * LOG2E


def _sbp_kernel(nv, qc, bias_ref, q_ref, k_ref, v_ref, o_ref, qb_ref, acc_ref, carry_ref):
    h = pl.program_id(1)
    s = pl.program_id(2)
    bias = bias_ref[h] * LOG2E
    scale = SB_DH ** -0.5 * LOG2E
    rows = qc * CHUNK
    u1 = _cumsum_matrix(CHUNK)
    u2 = _cumsum_matrix(2 * CHUNK)[:, :2 * CHUNK]

    def one_block(r0, nr, j, mask):
        start = j * CHUNK if isinstance(j, int) else pl.multiple_of(j * CHUNK, CHUNK)
        kb = k_ref[pl.ds(start, CHUNK), :].astype(BF16)
        vb = v_ref[pl.ds(start, CHUNK), :].astype(BF16)
        w = lax.dot_general(qb_ref[r0:r0 + nr, :], kb, NT, preferred_element_type=F32) + bias
        cost = _stay_cost2(w)
        r = jnp.dot(jnp.where(mask, cost, 0.0).astype(BF16), u1, preferred_element_type=F32)
        c = carry_ref[r0:r0 + nr, :]
        a = jnp.where(mask, jnp.exp2(w - (cost + r[:, :CHUNK] + c)), 0.0)
        acc_ref[r0:r0 + nr, :] += jnp.dot(a.astype(BF16), vb, preferred_element_type=F32)
        carry_ref[r0:r0 + nr, :] = c + r[:, CHUNK:]

    def two_blocks(j_hi, r0=0, mask=None):
        start = pl.multiple_of((j_hi - 1) * CHUNK, CHUNK)
        kb = k_ref[pl.ds(start, 2 * CHUNK), :].astype(BF16)
        vb = v_ref[pl.ds(start, 2 * CHUNK), :].astype(BF16)
        w = lax.dot_general(qb_ref[r0:rows, :], kb, NT, preferred_element_type=F32) + bias
        cost = _stay_cost2(w)
        if mask is not None:
            cost = jnp.where(mask, cost, 0.0)
        r = jnp.dot(cost.astype(BF16), u2, preferred_element_type=F32)
        c = carry_ref[r0:rows, :]
        a = jnp.exp2(w - (cost + r + jnp.concatenate([c, c], axis=1)))
        if mask is not None:
            a = jnp.where(mask, a, 0.0)
        acc_ref[r0:rows, :] += jnp.dot(a.astype(BF16), vb, preferred_element_type=F32)
        carry_ref[r0:rows, :] = c + (r[:, 0:1] + cost[:, 0:1])

    ci = lax.broadcasted_iota(jnp.int32, (CHUNK, CHUNK), 1)
    ri = lax.broadcasted_iota(jnp.int32, (CHUNK, CHUNK), 0)

    @pl.when(s == 0)
    def _():
        qb_ref[0:CHUNK, :] = (q_ref[0:CHUNK, :] * scale).astype(BF16)
        acc_ref[0:CHUNK, :] = jnp.zeros((CHUNK, SB_DH), F32)
        carry_ref[0:CHUNK, :] = jnp.zeros((CHUNK, CHUNK), F32)
        one_block(0, CHUNK, 0, jnp.logical_and(ci < ri, ci < nv))
        o_ref[0:CHUNK, :] = acc_ref[0:CHUNK, :].astype(BF16)

    @pl.when(s > 0)
    def _():
        jb = 1 + (s - 1) * qc
        q0 = pl.multiple_of(jb * CHUNK, CHUNK)
        qb_ref[...] = (q_ref[pl.ds(q0, rows), :] * scale).astype(BF16)
        acc_ref[...] = jnp.zeros((rows, SB_DH), F32)
        carry_ref[...] = jnp.zeros((rows, CHUNK), F32)
        for p in range(qc // 2 - 1, -1, -1):
            nr = rows - 2 * p * CHUNK
            kk = lax.broadcasted_iota(jnp.int32, (nr, 2 * CHUNK), 1)
            rr = lax.broadcasted_iota(jnp.int32, (nr, 2 * CHUNK), 0)
            two_blocks(jb + 2 * p + 1, 2 * p * CHUNK, kk < rr)

        pairs = qc // 2

        def body(t, carry):
            for p in range(pairs):
                two_blocks(jb - 1 - 2 * (pairs * t + p))
            return carry

        lax.fori_loop(0, ((s - 1) * qc) // (2 * pairs), body, 0)
        ca = lax.broadcasted_iota(jnp.int32, (rows, CHUNK), 1)
        one_block(0, rows, 0, ca < nv)
        o_ref[pl.ds(q0, rows), :] = acc_ref[...].astype(BF16)


def _sb_prompt(proj, bias, nb, nc, nv):
    m = proj.shape[0]
    t_pad = nc * CHUNK
    qc = next(c for c in (8, 4, 2) if (nc - 1) % c == 0)
    rows = qc * CHUNK

    def seq(off):
        return pl.BlockSpec((t_pad, SB_DH), lambda b, h, s: (b, off // SB_DH + h))

    return pl.pallas_call(
        functools.partial(_sbp_kernel, nv, qc),
        grid=(nb, SB_HEADS, 1 + (nc - 1) // qc),
        in_specs=[pl.BlockSpec(memory_space=pltpu.SMEM), seq(COL_SQ), seq(COL_SK), seq(COL_SV)],
        out_specs=pl.BlockSpec((t_pad, SB_DH), lambda b, h, s: (b, h)),
        out_shape=jax.ShapeDtypeStruct((m, SB_HEADS * SB_DH), BF16),
        scratch_shapes=[pltpu.VMEM((rows, SB_DH), BF16), pltpu.VMEM((rows, SB_DH), F32),
                        pltpu.VMEM((rows, CHUNK), F32)],
        compiler_params=_params(("parallel", "parallel", "arbitrary")),
        name="sb_prompt",
    )(bias, proj, proj, proj)


SBS_PAGES = 16
HEAD_BITS = SB_HEADS.bit_length() - 1
assert 1 << HEAD_BITS == SB_HEADS


def _head_of(x):
    return jnp.bitwise_and(x, SB_HEADS - 1)


def _sbs_kernel(lc, n_steps, past, dec_len, pt_ref, bias_ref, q_ref, kn_ref, vn_ref, u_ref, *refs):
    del pt_ref
    pp = SBS_PAGES
    k_refs, v_refs = refs[:pp], refs[pp:2 * pp]
    o_ref, acc_ref, carry_ref = refs[2 * pp:]
    s = pl.program_id(1)
    rows = SB_HEADS * pp
    page_w = k_refs[0].shape[0]
    scale = SB_DH ** -0.5
    rr = lax.broadcasted_iota(jnp.int32, (rows, page_w), 0)
    nn = lax.broadcasted_iota(jnp.int32, (rows, page_w), 1)
    r8 = lax.broadcasted_iota(jnp.int32, (8, SB_DH), 0)

    @pl.when(s == 0)
    def _():
        kpos = past + lax.broadcasted_iota(jnp.int32, (1, 1), 1)
        visible = kpos < past + dec_len - 1
        acc0 = jnp.zeros((8, SB_DH), F32)
        carry0 = jnp.zeros((8, LANES), F32)
        for h in range(SB_HEADS):
            sl = slice(h * SB_DH, (h + 1) * SB_DH)
            z_new = (jnp.sum(q_ref[0:1, sl] * kn_ref[0:1, sl], axis=1, keepdims=True) * scale
                     + bias_ref[h])
            cost_new = _softplus(z_new)
            a_new = jnp.where(visible, jnp.exp(z_new - cost_new), 0.0)
            acc0 = jnp.where(r8 == h, a_new * vn_ref[0:1, sl], acc0)
            carry0 = jnp.where(_head_of(r8) == h, jnp.where(visible, cost_new, 0.0), carry0)
        acc_ref[...] = acc0
        carry_ref[...] = carry0

    r_q = lax.broadcasted_iota(jnp.int32, (rows, SB_DH), 0)
    qrep = jnp.zeros((rows, SB_DH), F32)
    for h in range(SB_HEADS):
        qrep = jnp.where(_head_of(r_q) == h, q_ref[0:1, h * SB_DH:(h + 1) * SB_DH], qrep)

    z = jnp.zeros((rows, page_w), F32)
    for i in range(pp):
        qsel = jnp.where(lax.shift_right_logical(r_q, HEAD_BITS) == i, qrep, 0.0).astype(BF16)
        z = z + lax.dot_general(qsel, k_refs[i][...].astype(BF16), NT, preferred_element_type=F32)

    own = _head_of(nn) == _head_of(rr)
    bias = jnp.zeros((rows, page_w), F32)
    for h in range(SB_HEADS):
        bias = jnp.where(_head_of(rr) == h, bias_ref[h], bias)
    z = z * scale + bias
    cost = jnp.where(own, _softplus(z), 0.0)
    r = jnp.dot(cost.astype(BF16), u_ref[...], preferred_element_type=F32)
    rest, tot = r[:, :page_w], r[:, page_w:]

    offs = []
    base = carry_ref[...]
    for g in range(pp // 2):
        t8 = tot[8 * g:8 * g + 8]
        swapped = pltpu.roll(t8, SB_HEADS, 0)
        offs.append(base + jnp.where(r8 >= SB_HEADS, swapped, 0.0))
        base = base + t8 + swapped
    carry_ref[...] = base
    off = jnp.concatenate(offs, axis=0)
    off = jnp.concatenate([off] * (page_w // LANES), axis=1)
    a = jnp.where(own, jnp.exp(z - (cost + rest + off)), 0.0).astype(BF16)

    acc = acc_ref[...]
    for g in range(pp // 2):
        a8 = a[8 * g:8 * g + 8]
        o0 = jnp.dot(a8, v_refs[2 * g][...].astype(BF16), preferred_element_type=F32)
        o1 = jnp.dot(a8, v_refs[2 * g + 1][...].astype(BF16), preferred_element_type=F32)
        acc = acc + jnp.where(r8 < SB_HEADS, o0, o1)
    acc_ref[...] = acc

    @pl.when(s == n_steps - 1)
    def _():
        tot8 = acc_ref[...]
        o4 = tot8[0:SB_HEADS] + tot8[SB_HEADS:2 * SB_HEADS]
        row0 = lax.broadcasted_iota(jnp.int32, (lc, SB_DH), 0) == 0
        for h in range(SB_HEADS):
            o_ref[:, h * SB_DH:(h + 1) * SB_DH] = jnp.where(row0, o4[h:h + 1], 0.0).astype(BF16)


def _sb_sample(proj, bias, cache_k, cache_v, page_table, layer, nb, lc, dec_len):
    m = proj.shape[0]
    n_pages = page_table.shape[1]
    pp = SBS_PAGES
    assert n_pages % pp == 0 and pp % 2 == 0 and SB_HEADS * 2 == 8
    n_steps = n_pages // pp
    page_rows = cache_k.shape[2]
    past = n_pages * (page_rows // SB_HEADS)
    width = SB_HEADS * SB_DH
    assert SB_DH == LANES
    u = _cumsum_matrix(page_rows)

    def page_spec(i):
        return pl.BlockSpec((None, None, page_rows, SB_DH),
                            lambda b, s, pt: (layer, pt[b, n_pages - 1 - (s * pp + i)], 0, 0))

    def col(off):
        return pl.BlockSpec((lc, width), lambda b, s, pt: (b, off // width))

    grid_spec = pltpu.PrefetchScalarGridSpec(
        num_scalar_prefetch=1,
        grid=(nb, n_steps),
        in_specs=[pl.BlockSpec(memory_space=pltpu.SMEM), col(COL_SQ), col(COL_SK), col(COL_SV),
                  pl.BlockSpec(u.shape, lambda b, s, pt: (0, 0))]
                 + [page_spec(i) for i in range(pp)] + [page_spec(i) for i in range(pp)],
        out_specs=pl.BlockSpec((lc, width), lambda b, s, pt: (b, 0)),
        scratch_shapes=[pltpu.VMEM((8, SB_DH), F32), pltpu.VMEM((8, LANES), F32)],
    )
    return pl.pallas_call(
        functools.partial(_sbs_kernel, lc, n_steps, past, dec_len),
        grid_spec=grid_spec,
        out_shape=jax.ShapeDtypeStruct((m, width), BF16),
        compiler_params=_params(("parallel", "arbitrary")),
        name="sb_sample",
    )(page_table, bias, proj, proj, proj, u, *([cache_k] * pp), *([cache_v] * pp))


def _conv_taps(buf, w_ref, lc, taps):
    full = buf[0:8 + lc, :]
    out = full * w_ref[0:1, :]
    for k in range(1, taps):
        out = _shift_down_one(out) + full * w_ref[k:k + 1, :]
    return out[8:8 + lc, :]


def _shift_down_one(x):
    n, c = x.shape
    first = lax.broadcasted_iota(jnp.int32, (8, c), 0) == 0
    rots = [pltpu.roll(x[8 * i:8 * i + 8, :], 1, 0) for i in range(n // 8)]
    return jnp.concatenate([jnp.where(first, rots[i - 1], rots[i]) for i in range(n // 8)], axis=0)


def _roll_tail(buf, c, nv, lc, taps):
    base = 8 - (taps - 1)

    @pl.when(c == 0)
    def _():
        buf[base:8, :] = buf[base + nv:8 + nv, :]

    @pl.when(c > 0)
    def _():
        buf[base:8, :] = buf[base + lc:8 + lc, :]


def _ssd_kernel(nv, lc, n_chunks,
                xbc_ref, mz_ref, dt_ref, gb_ref, gc_ref, gh_ref,
                cw_ref, cb_ref, dtb_ref, alog_ref, dfull_ref, nw_ref, scw_ref,
                mc0_ref, h0_ref, sc0_ref,
                yssm_ref, ysc_ref, mc_ref, h_ref, sc_ref,
                cbuf, sbuf):
    c = pl.program_id(1)
    n_heads = h_ref.shape[0]
    m_inner = n_heads * M_HEADDIM
    rep = n_heads // M_GROUPS

    @pl.when(c == 0)
    def _():
        cbuf[0:8, :] = jnp.zeros((8, cbuf.shape[1]), F32)
        sbuf[0:8, :] = jnp.zeros((8, sbuf.shape[1]), F32)
        cbuf[8 - (M_CONV - 1):8, :] = mc0_ref[...]
        sbuf[8 - (SC_CONV - 1):8, :] = sc0_ref[...]
        h_ref[...] = h0_ref[...]

    cbuf[8:8 + lc, :] = xbc_ref[...]
    conv = _conv_taps(cbuf, cw_ref, lc, M_CONV) + cb_ref[...]
    _roll_tail(cbuf, c, nv, lc, M_CONV)
    xc = conv * _sigmoid(conv)
    mx = xc[:, :m_inner]
    mb = xc[:, m_inner:m_inner + M_GROUPS * M_STATE].astype(BF16)
    mc = xc[:, m_inner + M_GROUPS * M_STATE:].astype(BF16)

    row = lax.broadcasted_iota(jnp.int32, (lc, LANES), 0)
    valid = jnp.logical_or(c > 0, row < nv)
    dt = jnp.where(valid, _softplus(dt_ref[...] + dtb_ref[...]), 0.0)
    dta = dt * (-jnp.exp(alog_ref[...]))
    ii = lax.broadcasted_iota(jnp.int32, (lc, lc), 0)
    jj = lax.broadcasted_iota(jnp.int32, (lc, lc), 1)
    causal = ii >= jj
    la = jnp.dot(jnp.where(causal, 1.0, 0.0), dta, precision=HIGHEST, preferred_element_type=F32)

    def transposed(a):
        if lc < LANES:
            a = jnp.concatenate([a, jnp.zeros((LANES - lc, LANES), F32)], axis=0)
        return a.T[:, :lc]

    la_t = transposed(la)
    dt_t = transposed(dt)
    la_last = la[lc - 1:lc, :]
    ela = jnp.exp(la)
    w_end = jnp.exp(la_last - la) * dt
    e_last = jnp.exp(la_last)

    assert 2 * M_HEADDIM == LANES and rep % 2 == 0
    first_lanes = lax.broadcasted_iota(jnp.int32, (lc, LANES), 1) < M_HEADDIM
    first_rows = lax.broadcasted_iota(jnp.int32, (LANES, M_STATE), 0) < M_HEADDIM

    def per_head(a, h0, first):
        return jnp.where(first, a[:, h0:h0 + 1], a[:, h0 + 1:h0 + 2])

    ys = []
    for g in range(M_GROUPS):
        bm = mb[:, g * M_STATE:(g + 1) * M_STATE]
        cm = mc[:, g * M_STATE:(g + 1) * M_STATE]
        cb = lax.dot_general(cm, bm, NT, preferred_element_type=F32)
        for h0 in range(g * rep, (g + 1) * rep, 2):
            wts = []
            for hd in (h0, h0 + 1):
                seg = la[:, hd:hd + 1] - la_t[hd:hd + 1, :]
                dec = jnp.where(causal, jnp.exp(jnp.where(causal, seg, 0.0)), 0.0)
                wts.append((cb * dec * dt_t[hd:hd + 1, :]).astype(BF16))
            xp = mx[:, h0 * M_HEADDIM:(h0 + 2) * M_HEADDIM]
            x_diag = jnp.concatenate([jnp.where(first_lanes, xp, 0.0),
                                      jnp.where(first_lanes, 0.0, xp)], axis=0).astype(BF16)
            hst = h_ref[h0:h0 + 2].reshape(LANES, M_STATE)
            y2 = jnp.dot(jnp.concatenate(wts, axis=1), x_diag, preferred_element_type=F32)
            y2 = y2 + (lax.dot_general(cm, hst.astype(BF16), NT, preferred_element_type=F32)
                       * per_head(ela, h0, first_lanes))
            xw = (xp * per_head(w_end, h0, first_lanes)).astype(BF16)
            new = (per_head(e_last, h0, first_rows) * hst
                   + lax.dot_general(xw, bm, TN, preferred_element_type=F32))
            h_ref[h0:h0 + 2] = new.reshape(2, M_HEADDIM, M_STATE)
            ys.append(y2)
    y = jnp.concatenate(ys, axis=1) + dfull_ref[...] * mx
    mz = mz_ref[...]
    y = y * (mz * _sigmoid(mz))
    yssm_ref[...] = _rms(y, nw_ref[...]).astype(BF16)

    sbuf[8:8 + lc, :] = gc_ref[...] * gh_ref[...]
    ysc_ref[...] = (gb_ref[...] * _conv_taps(sbuf, scw_ref, lc, SC_CONV)).astype(BF16)
    _roll_tail(sbuf, c, nv, lc, SC_CONV)

    @pl.when(c == n_chunks - 1)
    def _():
        mc_ref[...] = cbuf[8 - (M_CONV - 1):8, :]
        sc_ref[...] = sbuf[8 - (SC_CONV - 1):8, :]


def _ssd(proj, dtp, cw, cb, dtb, alog, dfull, nw, scw, mc0, h0, sc0, nb, nc, lc, nv):
    m = proj.shape[0]
    n_heads, m_xbc, sc_w = h0.shape[1], cw.shape[1], scw.shape[1]
    m_inner = n_heads * M_HEADDIM

    def col(off, w):
        return pl.BlockSpec((lc, w), lambda b, c: (b * nc + c, off // w))

    def full(a):
        return pl.BlockSpec(a.shape, lambda b, c: (0,) * a.ndim)

    def per_seq(a):
        return pl.BlockSpec((None,) + a.shape[1:], lambda b, c: (b,) + (0,) * (a.ndim - 1))

    outs = [jax.ShapeDtypeStruct((m, m_inner), BF16), jax.ShapeDtypeStruct((m, sc_w), BF16),
            jax.ShapeDtypeStruct(mc0.shape, F32), jax.ShapeDtypeStruct(h0.shape, F32),
            jax.ShapeDtypeStruct(sc0.shape, F32)]
    return pl.pallas_call(
        functools.partial(_ssd_kernel, nv, lc, nc),
        grid=(nb, nc),
        in_specs=[col(COL_XBC, m_xbc), col(COL_MZ, m_inner),
                  pl.BlockSpec((lc, LANES), lambda b, c: (b * nc + c, 0)),
                  col(COL_GB, sc_w), col(COL_GC, sc_w), col(COL_GH, sc_w),
                  full(cw), full(cb), full(dtb), full(alog), full(dfull), full(nw), full(scw),
                  per_seq(mc0), per_seq(h0), per_seq(sc0)],
        out_specs=[pl.BlockSpec((lc, m_inner), lambda b, c: (b * nc + c, 0)),
                   pl.BlockSpec((lc, sc_w), lambda b, c: (b * nc + c, 0)),
                   per_seq(mc0), per_seq(h0), per_seq(sc0)],
        out_shape=outs,
        scratch_shapes=[pltpu.VMEM((8 + lc, m_xbc), F32), pltpu.VMEM((8 + lc, sc_w), F32)],
        compiler_params=_params(("parallel", "arbitrary")),
        name="ssd",
    )(proj, proj, dtp, proj, proj, proj, cw, cb, dtb, alog, dfull, nw, scw, mc0, h0, sc0)


def _merge_kernel(x_ref, gl_ref, yr_ref, yb_ref, ym_ref, yc_ref, wb_ref, wo_ref, nw_ref, out_ref):
    d = x_ref.shape[1]
    offs = [0]
    for y_ref in (yr_ref, yb_ref, ym_ref, yc_ref):
        offs.append(offs[-1] + y_ref.shape[1])
    merged = None
    for i, y_ref in enumerate((yr_ref, yb_ref, ym_ref, yc_ref)):
        br = jnp.dot(y_ref[...], wb_ref[offs[i]:offs[i + 1], :], preferred_element_type=F32)
        term = _sigmoid(gl_ref[:, i * d:(i + 1) * d]) * br
        merged = term if merged is None else merged + term
    o = jnp.dot(merged.astype(BF16), wo_ref[...], preferred_element_type=F32)
    out_ref[...] = x_ref[...] + _rms(o, nw_ref[...])


def _merge(x, proj, yr, yb, ym, yc, wb, wo, nw):
    m, d = x.shape
    tm = _pick_tile(m, 512, SAMPLE_CHUNK)

    def rows(a):
        return pl.BlockSpec((tm, a.shape[1]), lambda i: (i, 0))

    def full(a):
        return pl.BlockSpec(a.shape, lambda i: (0, 0))

    return pl.pallas_call(
        _merge_kernel,
        grid=(m // tm,),
        in_specs=[rows(x), pl.BlockSpec((tm, N_BRANCH * d), lambda i: (i, COL_GL // (N_BRANCH * d))),
                  rows(yr), rows(yb), rows(ym), rows(yc), full(wb), full(wo), full(nw)],
        out_specs=rows(x),
        out_shape=jax.ShapeDtypeStruct((m, d), F32),
        compiler_params=_params(("parallel",)),
        name="merge",
    )(x, proj, yr, yb, ym, yc, wb, wo, nw)


def _mlp_kernel(x_ref, nw_pre_ref, nw_post_ref, wu_ref, wd_ref, out_ref, h_scr, acc):
    j = pl.program_id(1)

    @pl.when(j == 0)
    def _():
        h_scr[...] = _rms(x_ref[...], nw_pre_ref[...]).astype(BF16)
        acc[...] = jnp.zeros_like(acc)

    f = jnp.dot(h_scr[...], wu_ref[...], preferred_element_type=F32)
    f = jnp.square(jnp.maximum(f, 0.0)).astype(BF16)
    acc[...] += jnp.dot(f, wd_ref[...], preferred_element_type=F32)

    @pl.when(j == pl.num_programs(1) - 1)
    def _():
        out_ref[...] = x_ref[...] + _rms(acc[...], nw_post_ref[...])


def _mlp(x, nw_pre, nw_post, wu, wd):
    m, d = x.shape
    d_ff = wu.shape[1]
    tm = _pick_tile(m, 1280, SAMPLE_CHUNK)
    tf = 1024
    return pl.pallas_call(
        _mlp_kernel,
        grid=(m // tm, d_ff // tf),
        in_specs=[pl.BlockSpec((tm, d), lambda i, j: (i, 0)),
                  pl.BlockSpec((1, d), lambda i, j: (0, 0)),
                  pl.BlockSpec((1, d), lambda i, j: (0, 0)),
                  pl.BlockSpec((d, tf), lambda i, j: (0, j)),
                  pl.BlockSpec((tf, d), lambda i, j: (j, 0))],
        out_specs=pl.BlockSpec((tm, d), lambda i, j: (i, 0)),
        out_shape=jax.ShapeDtypeStruct((m, d), F32),
        scratch_shapes=[pltpu.VMEM((tm, d), BF16), pltpu.VMEM((tm, d), F32)],
        compiler_params=_params(("parallel", "arbitrary")),
        name="mlp",
    )(x, nw_pre, nw_post, wu, wd)


def _reorder_w_in(w_in_l, d_model):
    sizes = (512, 512, 512, 512, 512, 512, 512, d_model, d_model + 2 * M_GROUPS * M_STATE,
             d_model // M_HEADDIM, d_model // 2, d_model // 2, d_model // 2, N_BRANCH * d_model)
    parts, off = [], 0
    for s in sizes:
        parts.append(w_in_l[:, off:off + s])
        off += s
    rq, rk, rv, rg, sq, sk, sv, mz, xbc, dt, gb, gc, gh, gl = parts
    w_main = jnp.concatenate([gl, xbc, mz, rq, rk, rv, rg, sq, sk, sv, gb, gc, gh], axis=1).astype(BF16)
    w_dt = jnp.pad(dt, ((0, 0), (0, LANES - dt.shape[1]))).astype(BF16)
    return w_main, w_dt


def _rope_tables(nc, lc, nv, pos0):
    r = jnp.arange(nc * lc)
    c, i = r // lc, r % lc
    pos = jnp.where(c == 0, i, nv + (c - 1) * lc + i) + pos0
    half = RET_DK // 2
    inv = ROPE_BASE ** (-jnp.arange(half, dtype=F32) / half)
    ang = pos.astype(F32)[:, None] * inv[None, :]
    cos, sin = jnp.cos(ang), jnp.sin(ang)
    return jnp.concatenate([cos, cos], axis=1), jnp.concatenate([-sin, sin], axis=1)


def _pad_lanes(v):
    return jnp.pad(v.astype(F32), (0, LANES - v.shape[0]))[None, :]


def _layer(x, l, nb, nc, lc, nv, pos0, sb_fn, states, wts):
    (w_in, w_branch, w_o, w_up, w_down, norm_w, ret_gn_w, m_conv_w, m_conv_b, m_dt_bias, m_a_log,
     m_d, m_norm_w, sc_w, sb_bias) = wts
    s_ret, s_ssm, s_mconv, s_sconv = states
    d = x.shape[1]
    w_main, w_dt = _reorder_w_in(w_in[l], d)
    proj, dtp = _inproj(x, norm_w[l, 0][None, :], w_main, w_dt)
    cos2, sin2 = _rope_tables(nc, lc, nv, pos0)
    y_ret, o_ret = _retention(proj, cos2, sin2, ret_gn_w[l][None, :], s_ret, nb, nc, lc, nv)
    y_sb = sb_fn(proj, sb_bias[l].astype(F32))
    dfull = jnp.repeat(m_d[l].astype(F32), M_HEADDIM)[None, :]
    y_ssm, y_sc, o_mconv, o_ssm, o_sconv = _ssd(
        proj, dtp, m_conv_w[l], m_conv_b[l][None, :], _pad_lanes(m_dt_bias[l]), _pad_lanes(m_a_log[l]),
        dfull, m_norm_w[l][None, :], sc_w[l], s_mconv, s_ssm, s_sconv, nb, nc, lc, nv)
    x = _merge(x, proj, y_ret, y_sb, y_ssm, y_sc, w_branch[l].astype(BF16), w_o[l].astype(BF16),
               norm_w[l, 1][None, :])
    x = _mlp(x, norm_w[l, 2][None, :], norm_w[l, 3][None, :], w_up[l].astype(BF16), w_down[l].astype(BF16))
    return x, proj, (o_ret, o_ssm, o_mconv, o_sconv)


def _kv_rows_kernel(nv, g, *refs):
    depth = (len(refs) - 2) // 2
    k_out, v_out = refs[-2:]
    layer = pl.program_id(0)
    c = pl.program_id(3)
    span = g * CHUNK
    for l in range(depth):
        for src, dst in ((refs[2 * l], k_out), (refs[2 * l + 1], v_out)):
            @pl.when(jnp.logical_and(layer == l, c == 0))
            def _(src=src, dst=dst):
                dst[0:nv, :] = src[0:nv, :]
                if g > 1:
                    dst[nv:nv + span - CHUNK, :] = src[CHUNK:span, :]

            @pl.when(jnp.logical_and(layer == l, c > 0))
            def _(src=src, dst=dst):
                start = pl.multiple_of(nv + c * span - CHUNK, 8)
                dst[pl.ds(start, span), :] = src[...]


def _kv_rows(projs, nb, nc, nv):
    depth = len(projs)
    t_valid = nv + (nc - 1) * CHUNK
    width = SB_HEADS * SB_DH
    half = width // 2
    assert nv % 8 == 0
    g = max(c for c in range(1, 17) if nc % c == 0)
    steps = nc // g

    def src(l, off):
        return pl.BlockSpec((g * CHUNK, half),
                            lambda d, b, h, c: (jnp.where(d == l, b * steps + c, 0), off // half + h))

    in_specs, args = [], []
    for l in range(depth):
        in_specs += [src(l, COL_SK), src(l, COL_SV)]
        args += [projs[l], projs[l]]
    out_spec = pl.BlockSpec((None, None, t_valid, half), lambda d, b, h, c: (d, b, 0, h))
    shape = jax.ShapeDtypeStruct((depth, nb, t_valid, width), F32)
    return pl.pallas_call(
        functools.partial(_kv_rows_kernel, nv, g),
        grid=(depth, nb, 2, steps),
        in_specs=in_specs,
        out_specs=[out_spec, out_spec],
        out_shape=[shape, shape],
        compiler_params=_params(("parallel", "parallel", "parallel", "arbitrary")),
        name="kv_rows",
    )(*args)


def kernel(x_prompt, x_sample, cache_sb_k, cache_sb_v, state_ret, state_ssm, state_mconv, state_sconv,
           page_table, meta, w_in, w_branch, w_o, w_up, w_down, norm_w, ret_gn_w, m_conv_w, m_conv_b,
           m_dt_bias, m_a_log, m_d, m_norm_w, sc_w, sb_bias):
    nb_p, seq, d = x_prompt.shape
    nb_s, dec_len, _ = x_sample.shape
    depth = w_in.shape[0]
    assert seq % CHUNK == 0 and dec_len == 1 and meta.shape[0] == N_META
    past = page_table.shape[1] * cache_sb_k.shape[2]
    wts = (w_in, w_branch, w_o, w_up, w_down, norm_w, ret_gn_w, m_conv_w, m_conv_b, m_dt_bias, m_a_log,
           m_d, m_norm_w, sc_w, sb_bias)

    nc_p = 1 + seq // CHUNK
    xp = jnp.concatenate([jnp.broadcast_to(meta.astype(F32)[None], (nb_p, N_META, d)),
                          jnp.zeros((nb_p, CHUNK - N_META, d), F32), x_prompt], axis=1)
    xp = xp.reshape(nb_p * nc_p * CHUNK, d)
    lc_s = SAMPLE_CHUNK
    xs = jnp.pad(x_sample, ((0, 0), (0, lc_s - dec_len), (0, 0))).reshape(nb_s * lc_s, d)

    n_phys, page = cache_sb_k.shape[1], cache_sb_k.shape[2]
    ck = cache_sb_k.reshape(depth, n_phys, page * SB_HEADS, SB_DH)
    cv = cache_sb_v.reshape(depth, n_phys, page * SB_HEADS, SB_DH)

    zeros_p = (jnp.zeros((nb_p,) + state_ret.shape[2:], F32), jnp.zeros((nb_p,) + state_ssm.shape[2:], F32),
               jnp.zeros((nb_p,) + state_mconv.shape[2:], F32), jnp.zeros((nb_p,) + state_sconv.shape[2:], F32))
    p_out = [[] for _ in range(4)]
    s_out = [[] for _ in range(6)]
    projs_p = []
    width = SB_HEADS * SB_DH
    for l in range(depth):
        xp, proj_p, st_p = _layer(
            xp, l, nb_p, nc_p, CHUNK, N_META, 0,
            lambda pr, bias: _sb_prompt(pr, bias, nb_p, nc_p, N_META), zeros_p, wts)
        xs, proj_s, st_s = _layer(
            xs, l, nb_s, 1, lc_s, dec_len, past,
            lambda pr, bias, l=l: _sb_sample(pr, bias, ck, cv, page_table, l, nb_s, lc_s, dec_len),
            (state_ret[l], state_ssm[l], state_mconv[l], state_sconv[l]), wts)
        projs_p.append(proj_p)
        p3 = proj_s.reshape(nb_s, lc_s, PROJ_W)[:, :dec_len]
        for i, off in enumerate((COL_SK, COL_SV)):
            s_out[i].append(p3[:, :, off:off + width].reshape(nb_s, dec_len, SB_HEADS, SB_DH))
        for i in range(4):
            p_out[i].append(st_p[i])
            s_out[2 + i].append(st_s[i])
    p_sb_k, p_sb_v = [a.reshape(a.shape[:3] + (SB_HEADS, SB_DH))
                      for a in _kv_rows(projs_p, nb_p, nc_p, N_META)]
    p_ret, p_ssm, p_mconv, p_sconv = [jnp.stack(a) for a in p_out]
    s_sb_k, s_sb_v, s_ret, s_ssm, s_mconv, s_sconv = [jnp.stack(a) for a in s_out]
    y_prompt = xp.reshape(nb_p, nc_p * CHUNK, d)[:, CHUNK:]
    y_sample = xs.reshape(nb_s, lc_s, d)[:, :dec_len]
    return (y_prompt, y_sample, p_sb_k, p_sb_v, p_ret, p_ssm, p_mconv, p_sconv,
            s_sb_k, s_sb_v, s_ret, s_ssm, s_mconv, s_sconv)
```
